```python
import jax
import jax.numpy as jnp
from jax import lax
import numpy as np

D_MODEL = 1024
BATCH = 8
SEQ = 2048
DEPTH = 2
DEC_BATCH = 8
DEC_SEQ = 4096
PAST_LEN = 128

N_EVEN = (DEPTH + 1) // 2
N_ODD = DEPTH // 2
CONV_DIM = D_MODEL // 2
CONV_K = 31
LRU_DIM = D_MODEL
LRU_HEADS = 16
LRU_HEAD_DIM = LRU_DIM // LRU_HEADS
LRU_CONV_K = 4
LRU_C = 8.0
IN_E = 2 * CONV_DIM + 2 * LRU_DIM
MIX_E = CONV_DIM + LRU_DIM
SGU_DIM = 2 * D_MODEL
SGU_HEADS = 16
SGU_HEAD_DIM = SGU_DIM // SGU_HEADS
CHUNK = 128
D_FF = 2816
N_EXPERTS = 8
TOP_K = 2
D_FF_EXPERT = 3584
EPS = 1e-6

kernel_name = 'hybrid_conv_rglru_sgu_moe_encoder'


def _rmsnorm(x, g):
    xf = x.astype(jnp.float32)
    y = xf * lax.rsqrt(jnp.mean(xf * xf, axis=-1, keepdims=True) + EPS)
    return (y * g.astype(jnp.float32)).astype(x.dtype)


def _layernorm(x, g, b):
    xf = x.astype(jnp.float32)
    mu = jnp.mean(xf, axis=-1, keepdims=True)
    var = jnp.mean(jnp.square(xf - mu), axis=-1, keepdims=True)
    y = (xf - mu) * lax.rsqrt(var + EPS) * g.astype(jnp.float32) + b.astype(jnp.float32)
    return y.astype(x.dtype)


def _dwconv(x, w, b, pad):
    y = lax.conv_general_dilated(x, w[:, None, :].astype(x.dtype), window_strides=(1,), padding=[pad], dimension_numbers=('NWC', 'WIO', 'NWC'), feature_group_count=x.shape[-1])
    return y + b.astype(x.dtype)


def _blockdiag(x, w, b):
    B, S, C = x.shape
    xr = x.reshape(B, S, LRU_HEADS, LRU_HEAD_DIM)
    y = jnp.einsum('bshi,hij->bshj', xr, w.astype(jnp.float32)) + b.astype(jnp.float32)
    return y.reshape(B, S, C)


def _lin_comb(left, right):
    a1, b1 = left
    a2, b2 = right
    return (a1 * a2, a2 * b1 + b2)


def _rglru(x, w_a, b_a, w_i, b_i, lam, reverse):
    xf = x.astype(jnp.float32)
    r = jax.nn.sigmoid(_blockdiag(xf, w_a, b_a))
    i = jax.nn.sigmoid(_blockdiag(xf, w_i, b_i))
    log_a = -LRU_C * r * jax.nn.softplus(-lam.astype(jnp.float32))
    a = jnp.exp(log_a)
    u = jnp.sqrt(jnp.maximum(-jnp.expm1(2.0 * log_a), 1e-12)) * (i * xf)
    _, h = lax.associative_scan(_lin_comb, (a, u), reverse=reverse, axis=1)
    return h


def _swiglu(h, w1, w3, w2):
    return (jax.nn.silu(h @ w1) * (h @ w3)) @ w2


def _even_mixer(h, p, j):
    dt = h.dtype
    z = h @ p['w_in_e'][j]
    za = z[..., :CONV_DIM]
    zg = z[..., CONV_DIM:2 * CONV_DIM]
    zy = z[..., 2 * CONV_DIM:2 * CONV_DIM + LRU_DIM]
    zx = z[..., 2 * CONV_DIM + LRU_DIM:]
    c = za * jax.nn.sigmoid(zg)
    c = _dwconv(c, p['dw_conv_w'][j], p['dw_conv_b'][j], (CONV_K // 2, CONV_K // 2))
    c = jax.nn.silu(_layernorm(c, p['conv_ln_g'][j], p['conv_ln_b'][j]))
    xfw = _dwconv(zx, p['rg_conv_w'][j, 0], p['rg_conv_b'][j, 0], (LRU_CONV_K - 1, 0))
    xbw = _dwconv(zx, p['rg_conv_w'][j, 1], p['rg_conv_b'][j, 1], (0, LRU_CONV_K - 1))
    hf = _rglru(xfw, p['rg_w_a'][j, 0], p['rg_b_a'][j, 0], p['rg_w_i'][j, 0], p['rg_b_i'][j, 0], p['rg_lam'][j, 0], False)
    hb = _rglru(xbw, p['rg_w_a'][j, 1], p['rg_b_a'][j, 1], p['rg_w_i'][j, 1], p['rg_b_i'][j, 1], p['rg_lam'][j, 1], True)
    r = (jax.nn.gelu(zy).astype(jnp.float32) * (hf + hb)).astype(dt)
    return jnp.concatenate([c, r], axis=-1) @ p['w_out_e'][j]


def _odd_mixer(h, p, j):
    dt = h.dtype
    z = jax.nn.gelu(h @ p['w_in_o'][j])
    u = z[..., :SGU_DIM]
    v = _layernorm(z[..., SGU_DIM:], p['sgu_ln_g'][j], p['sgu_ln_b'][j])
    B, S, _ = v.shape
    vc = v.reshape(B, S // CHUNK, CHUNK, SGU_HEADS, SGU_HEAD_DIM)
    bias = jnp.transpose(p['sgu_b'][j]).astype(dt)[None, None, :, :, None]
    sv = jnp.einsum('hpq,bnqhc->bnphc', p['sgu_w'][j].astype(dt), vc) + bias
    g = u * sv.reshape(B, S, SGU_DIM)
    return g @ p['w_out_o'][j]


def _moe(h, p, j):
    logits = h.astype(jnp.float32) @ p['w_router'][j].astype(jnp.float32)
    top_v, top_i = lax.top_k(logits, TOP_K)
    gates = jax.nn.softmax(top_v, axis=-1)
    combine = jnp.sum(jax.nn.one_hot(top_i, N_EXPERTS, dtype=jnp.float32) * gates[..., None], axis=-2)
    out = jnp.zeros(h.shape, jnp.float32)
    for e in range(N_EXPERTS):
        ye = _swiglu(h, p['moe_w1'][j, e], p['moe_w3'][j, e], p['moe_w2'][j, e])
        out = out + combine[..., e:e + 1] * ye.astype(jnp.float32)
    return out.astype(h.dtype)


def _trunk(x, p):
    h = x
    for layer in range(DEPTH):
        j = layer // 2
        if layer % 2 == 0:
            h = h + _even_mixer(_rmsnorm(h, p['ln_mix_e'][j]), p, j)
            h = h + _swiglu(_rmsnorm(h, p['ln_ffn_e'][j]), p['ffn_w1'][j], p['ffn_w3'][j], p['ffn_w2'][j])
        else:
            h = h + _odd_mixer(_rmsnorm(h, p['ln_mix_o'][j]), p, j)
            h = h + _moe(_rmsnorm(h, p['ln_ffn_o'][j]), p, j)
    return _rmsnorm(h, p['ln_final'])


def setup_inputs(seed: int = 0) -> dict:
    key = jax.random.key(seed)
    ks = jax.random.split(key, 40)
    f32 = jnp.float32

    def nrm(i, shape, scale):
        return jax.random.normal(ks[i], shape, f32) * scale

    a8 = jax.random.uniform(ks[14], (N_EVEN, 2, LRU_DIM), f32, minval=0.9, maxval=0.999)
    pa = a8 ** (1.0 / LRU_C)
    rg_lam = jnp.log(pa) - jnp.log1p(-pa)
    return {
        'x_prompt': nrm(0, (BATCH, SEQ, D_MODEL), 1.0),
        'x_sample': nrm(1, (DEC_BATCH, DEC_SEQ, D_MODEL), 1.0),
        'ln_mix_e': 1.0 + nrm(2, (N_EVEN, D_MODEL), 0.02),
        'w_in_e': nrm(3, (N_EVEN, D_MODEL, IN_E), D_MODEL ** -0.5),
        'dw_conv_w': nrm(4, (N_EVEN, CONV_K, CONV_DIM), CONV_K ** -0.5),
        'dw_conv_b': nrm(5, (N_EVEN, CONV_DIM), 0.02),
        'conv_ln_g': 1.0 + nrm(6, (N_EVEN, CONV_DIM), 0.02),
        'conv_ln_b': nrm(7, (N_EVEN, CONV_DIM), 0.02),
        'rg_conv_w': nrm(8, (N_EVEN, 2, LRU_CONV_K, LRU_DIM), LRU_CONV_K ** -0.5),
        'rg_conv_b': nrm(9, (N_EVEN, 2, LRU_DIM), 0.02),
        'rg_w_a': nrm(10, (N_EVEN, 2, LRU_HEADS, LRU_HEAD_DIM, LRU_HEAD_DIM), LRU_HEAD_DIM ** -0.5),
        'rg_b_a': nrm(11, (N_EVEN, 2, LRU_HEADS, LRU_HEAD_DIM), 0.02),
        'rg_w_i': nrm(12, (N_EVEN, 2, LRU_HEADS, LRU_HEAD_DIM, LRU_HEAD_DIM), LRU_HEAD_DIM ** -0.5),
        'rg_b_i': nrm(13, (N_EVEN, 2, LRU_HEADS, LRU_HEAD_DIM), 0.02),
        'rg_lam': rg_lam,
        'w_out_e': nrm(15, (N_EVEN, MIX_E, D_MODEL), MIX_E ** -0.5),
        'ln_ffn_e': 1.0 + nrm(16, (N_EVEN, D_MODEL), 0.02),
        'ffn_w1': nrm(17, (N_EVEN, D_MODEL, D_FF), D_MODEL ** -0.5),
        'ffn_w3': nrm(18, (N_EVEN, D_MODEL, D_FF), D_MODEL ** -0.5),
        'ffn_w2': nrm(19, (N_EVEN, D_FF, D_MODEL), D_FF ** -0.5),
        'ln_mix_o': 1.0 + nrm(20, (N_ODD, D_MODEL), 0.02),
        'w_in_o': nrm(21, (N_ODD, D_MODEL, 2 * SGU_DIM), D_MODEL ** -0.5),
        'sgu_ln_g': 1.0 + nrm(22, (N_ODD, SGU_DIM), 0.02),
        'sgu_ln_b': nrm(23, (N_ODD, SGU_DIM), 0.02),
        'sgu_w': nrm(24, (N_ODD, SGU_HEADS, CHUNK, CHUNK), CHUNK ** -0.5),
        'sgu_b': 1.0 + nrm(25, (N_ODD, SGU_HEADS, CHUNK), 0.02),
        'w_out_o': nrm(26, (N_ODD, SGU_DIM, D_MODEL), SGU_DIM ** -0.5),
        'ln_ffn_o': 1.0 + nrm(27, (N_ODD, D_MODEL), 0.02),
        'w_router': nrm(28, (N_ODD, D_MODEL, N_EXPERTS), D_MODEL ** -0.5),
        'moe_w1': nrm(29, (N_ODD, N_EXPERTS, D_MODEL, D_FF_EXPERT), D_MODEL ** -0.5),
        'moe_w3': nrm(30, (N_ODD, N_EXPERTS, D_MODEL, D_FF_EXPERT), D_MODEL ** -0.5),
        'moe_w2': nrm(31, (N_ODD, N_EXPERTS, D_FF_EXPERT, D_MODEL), D_FF_EXPERT ** -0.5),
        'ln_final': 1.0 + nrm(32, (D_MODEL,), 0.02),
    }


def reference(x_prompt, x_sample, ln_mix_e, w_in_e, dw_conv_w, dw_conv_b, conv_ln_g, conv_ln_b, rg_conv_w, rg_conv_b, rg_w_a, rg_b_a, rg_w_i, rg_b_i, rg_lam, w_out_e, ln_ffn_e, ffn_w1, ffn_w3, ffn_w2, ln_mix_o, w_in_o, sgu_ln_g, sgu_ln_b, sgu_w, sgu_b, w_out_o, ln_ffn_o, w_router, moe_w1, moe_w3, moe_w2, ln_final):
    p = {
        'ln_mix_e': ln_mix_e, 'w_in_e': w_in_e, 'dw_conv_w': dw_conv_w, 'dw_conv_b': dw_conv_b,
        'conv_ln_g': conv_ln_g, 'conv_ln_b': conv_ln_b, 'rg_conv_w': rg_conv_w, 'rg_conv_b': rg_conv_b,
        'rg_w_a': rg_w_a, 'rg_b_a': rg_b_a, 'rg_w_i': rg_w_i, 'rg_b_i': rg_b_i, 'rg_lam': rg_lam,
        'w_out_e': w_out_e, 'ln_ffn_e': ln_ffn_e, 'ffn_w1': ffn_w1, 'ffn_w3': ffn_w3, 'ffn_w2': ffn_w2,
        'ln_mix_o': ln_mix_o, 'w_in_o': w_in_o, 'sgu_ln_g': sgu_ln_g, 'sgu_ln_b': sgu_ln_b,
        'sgu_w': sgu_w, 'sgu_b': sgu_b, 'w_out_o': w_out_o, 'ln_ffn_o': ln_ffn_o,
        'w_router': w_router, 'moe_w1': moe_w1, 'moe_w3': moe_w3, 'moe_w2': moe_w2,
        'ln_final': ln_final,
    }
    y_prompt = _trunk(x_prompt, p)
    y_sample = _trunk(x_sample, p)
    return (y_prompt, y_sample)
```

```python
import functools

import jax
import jax.numpy as jnp
from jax import lax
from jax.experimental import pallas as pl
from jax.experimental.pallas import tpu as pltpu

EPS = 1e-6
LRU_C = 8.0
LANES = 128
SUBLANES = 8
MXU_DIM = 256
VMEM_LIMIT_BYTES = 56 * 1024 * 1024
F32 = jnp.float32
BF16 = jnp.bfloat16


def _params(*sem):
    return pltpu.CompilerParams(dimension_semantics=sem, vmem_limit_bytes=VMEM_LIMIT_BYTES)


def _sigmoid(x):
    return 0.5 * (jnp.tanh(0.5 * x) + 1.0)


def _gelu_tanh(x):
    c = 0.7978845608028654
    return 0.5 * x * (1.0 + jnp.tanh(c * (x + 0.044715 * (x * x * x))))


def _rmsnorm_f32(x, g):
    return x * lax.rsqrt(jnp.mean(x * x, axis=-1, keepdims=True) + EPS) * g


def _pick_tile(n, want):
    t = min(n, want)
    while n % t:
        t //= 2
    return t


def _inproj_e_kernel(x_ref, g_ref, w_ref, o_ref, *, n_chunk):
    xn = _rmsnorm_f32(x_ref[...], g_ref[...]).astype(BF16)
    n = o_ref.shape[-1]
    for j in range(n // n_chunk):
        cols = slice(j * n_chunk, (j + 1) * n_chunk)
        o_ref[:, cols] = jnp.dot(xn, w_ref[:, cols], preferred_element_type=F32).astype(o_ref.dtype)


def _inproj_e(x, g, w):
    B, S, D = x.shape
    N = w.shape[1]
    tt = _pick_tile(S, 512)
    out = pl.pallas_call(
        functools.partial(_inproj_e_kernel, n_chunk=1024),
        grid=(S // tt, B),
        in_specs=[
            pl.BlockSpec((None, tt, D), lambda i, b: (b, i, 0)),
            pl.BlockSpec((1, D), lambda i, b: (0, 0)),
            pl.BlockSpec((D, N), lambda i, b: (0, 0)),
        ],
        out_specs=pl.BlockSpec((tt, N), lambda i, b: (i, b)),
        out_shape=jax.ShapeDtypeStruct((S, B * N), BF16),
        compiler_params=_params("parallel", "parallel"),
        name="inproj_even",
    )(x, g.reshape(1, D), w)
    return out.reshape(S * B, N)


def _conv_kernel(zp_ref, zm_ref, zn_ref, w_ref, b_ref, g_ref, be_ref, o_ref, s_ref, y_ref,
                 *, rows, halo, cdim, ktaps, batch, unroll, ln_rows):
    i = pl.program_id(0)
    n = pl.num_programs(0)

    def glu(z):
        return z[:, :cdim].astype(F32) * _sigmoid(z[:, cdim:].astype(F32))

    s_ref[halo:halo + rows, :] = glu(zm_ref[...])
    s_ref[0:halo, :] = jnp.where(i > 0, glu(zp_ref[...]), 0.0)
    s_ref[halo + rows:, :] = jnp.where(i < n - 1, glu(zn_ref[...]), 0.0)

    pad = ktaps // 2
    base = halo - pad * batch
    step = batch * unroll
    for j in range(cdim // LANES):
        lanes = slice(j * LANES, (j + 1) * LANES)
        wj = [jnp.broadcast_to(w_ref[k:k + 1, lanes], (batch, LANES)) for k in range(ktaps)]
        bj = jnp.broadcast_to(b_ref[:, lanes], (batch, LANES))

        def body(ci, carry, lanes=lanes, wj=wj, bj=bj):
            r0 = pl.multiple_of(ci * step, step)
            accs = [bj] * unroll
            for m in range(unroll + ktaps - 1):
                xm = s_ref[pl.ds(r0 + base + batch * m, batch), lanes]
                for u in range(unroll):
                    k = m - u
                    if 0 <= k < ktaps:
                        accs[u] = accs[u] + wj[k] * xm
            for u in range(unroll):
                y_ref[pl.ds(r0 + batch * u, batch), lanes] = accs[u]
            return carry

        lax.fori_loop(0, rows // step, body, 0)

    def ln_body(ci, carry):
        r0 = pl.multiple_of(ci * ln_rows, ln_rows)
        y = y_ref[pl.ds(r0, ln_rows), :]
        mu = jnp.mean(y, axis=-1, keepdims=True)
        d = y - mu
        var = jnp.mean(d * d, axis=-1, keepdims=True)
        yn = d * lax.rsqrt(var + EPS) * g_ref[...] + be_ref[...]
        o_ref[pl.ds(r0, ln_rows), :] = (yn * _sigmoid(yn)).astype(o_ref.dtype)
        return carry

    lax.fori_loop(0, rows // ln_rows, ln_body, 0)


def _conv_branch(z, w, b, ln_g, ln_b, batch):
    R = z.shape[0]
    ktaps, cdim = w.shape
    halo = 128
    assert (ktaps // 2) * batch <= halo and batch == SUBLANES
    rows = _pick_tile(R, 2048)
    assert rows % halo == 0
    hb = rows // halo
    nh = R // halo
    vec = lambda a: a.reshape(1, cdim)
    return pl.pallas_call(
        functools.partial(_conv_kernel, rows=rows, halo=halo, cdim=cdim, ktaps=ktaps, batch=batch,
                          unroll=8, ln_rows=min(rows, 256)),
        grid=(R // rows,),
        in_specs=[
            pl.BlockSpec((halo, 2 * cdim), lambda i: (jnp.maximum(i * hb - 1, 0), 0)),
            pl.BlockSpec((rows, 2 * cdim), lambda i: (i, 0)),
            pl.BlockSpec((halo, 2 * cdim), lambda i: (jnp.minimum((i + 1) * hb, nh - 1), 0)),
            pl.BlockSpec((ktaps, cdim), lambda i: (0, 0)),
            pl.BlockSpec((1, cdim), lambda i: (0, 0)),
            pl.BlockSpec((1, cdim), lambda i: (0, 0)),
            pl.BlockSpec((1, cdim), lambda i: (0, 0)),
        ],
        out_specs=pl.BlockSpec((rows, cdim), lambda i: (i, 0)),
        out_shape=jax.ShapeDtypeStruct((R, cdim), BF16),
        scratch_shapes=[pltpu.VMEM((rows + 2 * halo, cdim), F32), pltpu.VMEM((rows, cdim), F32)],
        compiler_params=_params("parallel"),
        name="conv_branch",
    )(z, z, z, w, vec(b), vec(ln_g), vec(ln_b))


def _lru_kernel(*refs, rows, batch, ktaps, sub, reverse, final):
    if final:
        (zx_ref, zy_ref, hb_ref, cw_ref, cb_ref, wg_ref, ba_ref, bi_ref, lam_ref,
         o_ref, xs_ref, a_ref, u_ref, h_ref) = refs
    else:
        (zx_ref, cw_ref, cb_ref, wg_ref, ba_ref, bi_ref, lam_ref,
         o_ref, xs_ref, a_ref, u_ref, h_ref) = refs
    hl = (ktaps - 1) * batch
    main = 0 if reverse else hl
    ldim = zx_ref.shape[-1]
    ngrp = wg_ref.shape[0]
    gw = ldim // ngrp

    @pl.when(pl.program_id(0) == 0)
    def _():
        h_ref[...] = jnp.zeros_like(h_ref)
        xs_ref[...] = jnp.zeros_like(xs_ref)

    xs_ref[main:main + rows, :] = zx_ref[...].astype(F32)

    lam = lam_ref[...]
    sp = jnp.maximum(-lam, 0.0) + jnp.log(1.0 + jnp.exp(-jnp.abs(lam)))
    neg_c_sp = -LRU_C * sp

    def gate_body(ci, carry):
        r0 = pl.multiple_of(ci * sub, sub)
        xc = jnp.broadcast_to(cb_ref[...], (sub, ldim))
        for k in range(ktaps):
            xc = xc + cw_ref[k:k + 1, :] * xs_ref[pl.ds(r0 + batch * k, sub), :]
        xcb = xc.astype(BF16)
        for g in range(ngrp):
            cols = slice(g * gw, (g + 1) * gw)
            gates = jnp.dot(xcb[:, cols], wg_ref[g], preferred_element_type=F32)
            r = _sigmoid(gates[:, :gw] + ba_ref[:, cols])
            ig = _sigmoid(gates[:, gw:] + bi_ref[:, cols])
            log_a = r * neg_c_sp[:, cols]
            a_ref[pl.ds(r0, sub), cols] = jnp.exp(log_a)
            th = jnp.tanh(log_a)
            mult = jnp.sqrt(jnp.maximum(-2.0 * th / (1.0 - th), 1e-12))
            u_ref[pl.ds(r0, sub), cols] = mult * (ig * xc[:, cols])
        return carry

    lax.fori_loop(0, rows // sub, gate_body, 0)

    if reverse:
        xs_ref[rows:rows + hl, :] = xs_ref[0:hl, :]
    else:
        xs_ref[0:hl, :] = xs_ref[rows:rows + hl, :]

    nt = rows // batch

    def scan_body(t, h):
        tt = (nt - 1 - t) if reverse else t
        r = pl.multiple_of(tt * batch, batch)
        h = a_ref[pl.ds(r, batch), :] * h + u_ref[pl.ds(r, batch), :]
        u_ref[pl.ds(r, batch), :] = h
        return h

    h_ref[...] = lax.fori_loop(0, nt, scan_body, h_ref[...], unroll=8)

    def out_body(ci, carry):
        r0 = pl.multiple_of(ci * sub, sub)
        h = u_ref[pl.ds(r0, sub), :]
        if final:
            h = h + hb_ref[pl.ds(r0, sub), :].astype(F32)
            h = _gelu_tanh(zy_ref[pl.ds(r0, sub), :].astype(F32)) * h
        o_ref[pl.ds(r0, sub), :] = h.astype(o_ref.dtype)
        return carry

    lax.fori_loop(0, rows // sub, out_body, 0)


def _pack_blockdiag(w, width):
    H, dh, _ = w.shape
    per = width // dh
    G = H // per
    w = w.reshape(G, per, dh, dh)
    eye = jnp.eye(per, dtype=w.dtype)
    return jnp.einsum("pq,gpij->gpiqj", eye, w).reshape(G, per * dh, per * dh)


def _lru_pass(z, hb, cw, cb, w_a, b_a, w_i, b_i, lam, *, batch, reverse, col_x, col_y):
    R = z.shape[0]
    ktaps, ldim = cw.shape
    final = hb is not None
    rows = _pick_tile(R, 1024)
    nblk = R // rows
    wg = jnp.concatenate([_pack_blockdiag(w_a, MXU_DIM), _pack_blockdiag(w_i, MXU_DIM)], axis=-1).astype(BF16)
    ngrp = wg.shape[0]
    vec = lambda a: a.reshape(1, ldim).astype(F32)
    blk = (lambda i: nblk - 1 - i) if reverse else (lambda i: i)
    const2 = lambda i: (0, 0)
    in_specs = [pl.BlockSpec((rows, ldim), lambda i: (blk(i), col_x))]
    args = [z]
    if final:
        in_specs += [pl.BlockSpec((rows, ldim), lambda i: (blk(i), col_y)),
                     pl.BlockSpec((rows, ldim), lambda i: (blk(i), 0))]
        args += [z, hb]
    in_specs += [
        pl.BlockSpec((ktaps, ldim), const2),
        pl.BlockSpec((1, ldim), const2),
        pl.BlockSpec((ngrp, MXU_DIM, 2 * MXU_DIM), lambda i: (0, 0, 0)),
        pl.BlockSpec((1, ldim), const2),
        pl.BlockSpec((1, ldim), const2),
        pl.BlockSpec((1, ldim), const2),
    ]
    args += [cw, vec(cb), wg, vec(b_a), vec(b_i), vec(lam)]
    hl = (ktaps - 1) * batch
    return pl.pallas_call(
        functools.partial(_lru_kernel, rows=rows, batch=batch, ktaps=ktaps, sub=min(rows, 256),
                          reverse=reverse, final=final),
        grid=(nblk,),
        in_specs=in_specs,
        out_specs=pl.BlockSpec((rows, ldim), lambda i: (blk(i), 0)),
        out_shape=jax.ShapeDtypeStruct((R, ldim), BF16),
        scratch_shapes=[
            pltpu.VMEM((rows + hl, ldim), F32),
            pltpu.VMEM((rows, ldim), F32),
            pltpu.VMEM((rows, ldim), F32),
            pltpu.VMEM((batch, ldim), F32),
        ],
        compiler_params=_params("arbitrary"),
        name="rglru_bwd" if reverse else "rglru_fwd",
    )(*args)


def _outproj_e_kernel(c_ref, r_ref, x_ref, w_ref, o_ref, *, cdim):
    acc = jnp.dot(c_ref[...], w_ref[:cdim, :], preferred_element_type=F32)
    acc = acc + jnp.dot(r_ref[...], w_ref[cdim:, :], preferred_element_type=F32)
    o_ref[...] = x_ref[...] + acc


def _outproj_e(c, r, x, w):
    B, S, D = x.shape
    cdim, ldim = c.shape[1], r.shape[1]
    tt = _pick_tile(S, 512)
    return pl.pallas_call(
        functools.partial(_outproj_e_kernel, cdim=cdim),
        grid=(S // tt, B),
        in_specs=[
            pl.BlockSpec((tt, cdim), lambda i, b: (i, b)),
            pl.BlockSpec((tt, ldim), lambda i, b: (i, b)),
            pl.BlockSpec((None, tt, D), lambda i, b: (b, i, 0)),
            pl.BlockSpec((cdim + ldim, D), lambda i, b: (0, 0)),
        ],
        out_specs=pl.BlockSpec((None, tt, D), lambda i, b: (b, i, 0)),
        out_shape=jax.ShapeDtypeStruct((B, S, D), F32),
        compiler_params=_params("parallel", "parallel"),
        name="outproj_even",
    )(c.reshape(S, B * cdim), r.reshape(S, B * ldim), x, w)


def _ffn_kernel(x_ref, g_ref, w1_ref, w3_ref, w2_ref, o_ref, xn_ref):
    f = pl.program_id(1)

    @pl.when(f == 0)
    def _():
        x = x_ref[...]
        xn_ref[...] = _rmsnorm_f32(x, g_ref[...]).astype(BF16)
        o_ref[...] = x

    xn = xn_ref[...]
    a = jnp.dot(xn, w1_ref[...], preferred_element_type=F32)
    b = jnp.dot(xn, w3_ref[...], preferred_element_type=F32)
    act = (a * _sigmoid(a) * b).astype(BF16)
    o_ref[...] += jnp.dot(act, w2_ref[...], preferred_element_type=F32)


def _ffn(x, g, w1, w3, w2, *, tm_want=1024, tf_want=1408):
    T, D = x.shape
    F = w1.shape[1]
    tm = _pick_tile(T, tm_want)
    tf = tf_want if F % tf_want == 0 else _pick_tile(F, 256)
    return pl.pallas_call(
        _ffn_kernel,
        grid=(T // tm, F // tf),
        in_specs=[
            pl.BlockSpec((tm, D), lambda i, f: (i, 0)),
            pl.BlockSpec((1, D), lambda i, f: (0, 0)),
            pl.BlockSpec((D, tf), lambda i, f: (0, f)),
            pl.BlockSpec((D, tf), lambda i, f: (0, f)),
            pl.BlockSpec((tf, D), lambda i, f: (f, 0)),
        ],
        out_specs=pl.BlockSpec((tm, D), lambda i, f: (i, 0)),
        out_shape=jax.ShapeDtypeStruct((T, D), F32),
        scratch_shapes=[pltpu.VMEM((tm, D), BF16)],
        compiler_params=_params("parallel", "arbitrary"),
        name="ffn_swiglu",
    )(x, g.reshape(1, D), w1, w3, w2)


def _inproj_o_kernel(x_ref, g_ref, w_ref, lg_ref, lb_ref, u_ref, v_ref, *, sdim):
    xn = _rmsnorm_f32(x_ref[...], g_ref[...]).astype(BF16)
    zu = _gelu_tanh(jnp.dot(xn, w_ref[:, :sdim], preferred_element_type=F32))
    u_ref[...] = zu.astype(u_ref.dtype)
    zv = _gelu_tanh(jnp.dot(xn, w_ref[:, sdim:], preferred_element_type=F32))
    mu = jnp.mean(zv, axis=-1, keepdims=True)
    d = zv - mu
    var = jnp.mean(d * d, axis=-1, keepdims=True)
    v_ref[...] = (d * lax.rsqrt(var + EPS) * lg_ref[...] + lb_ref[...]).astype(v_ref.dtype)


def _inproj_o(x, g, w, ln_g, ln_b):
    T, D = x.shape
    sdim = w.shape[1] // 2
    tm = _pick_tile(T, 512)
    row = lambda i: (i, 0)
    const = lambda i: (0, 0)
    return pl.pallas_call(
        functools.partial(_inproj_o_kernel, sdim=sdim),
        grid=(T // tm,),
        in_specs=[
            pl.BlockSpec((tm, D), row),
            pl.BlockSpec((1, D), const),
            pl.BlockSpec((D, 2 * sdim), const),
            pl.BlockSpec((1, sdim), const),
            pl.BlockSpec((1, sdim), const),
        ],
        out_specs=[pl.BlockSpec((tm, sdim), row), pl.BlockSpec((tm, sdim), row)],
        out_shape=[jax.ShapeDtypeStruct((T, sdim), BF16), jax.ShapeDtypeStruct((T, sdim), BF16)],
        compiler_params=_params("parallel"),
        name="inproj_odd",
    )(x, g.reshape(1, D), w, ln_g.reshape(1, sdim), ln_b.reshape(1, sdim))


def _sgu_out_kernel(u_ref, v_ref, sw_ref, sb_ref, w_ref, x_ref, o_ref, g_ref, *, chunk, heads):
    tm, sdim = u_ref.shape
    hd = sdim // heads
    for n in range(tm // chunk):
        rows = slice(n * chunk, (n + 1) * chunk)
        for h in range(heads):
            lanes = slice(h * hd, (h + 1) * hd)
            sv = jnp.dot(sw_ref[h], v_ref[rows, lanes], preferred_element_type=F32) + sb_ref[h]
            g_ref[rows, lanes] = (u_ref[rows, lanes].astype(F32) * sv).astype(BF16)
    o_ref[...] = x_ref[...] + jnp.dot(g_ref[...], w_ref[...], preferred_element_type=F32)


def _sgu_out(u, v, sgu_w, sgu_b, w, x):
    T, sdim = u.shape
    D = x.shape[1]
    heads, chunk, _ = sgu_w.shape
    hd = sdim // heads
    tm = _pick_tile(T, 512)
    assert tm % chunk == 0
    sb = jnp.broadcast_to(sgu_b.astype(F32)[:, :, None], (heads, chunk, hd))
    row = lambda i: (i, 0)
    return pl.pallas_call(
        functools.partial(_sgu_out_kernel, chunk=chunk, heads=heads),
        grid=(T // tm,),
        in_specs=[
            pl.BlockSpec((tm, sdim), row),
            pl.BlockSpec((tm, sdim), row),
            pl.BlockSpec((heads, chunk, chunk), lambda i: (0, 0, 0)),
            pl.BlockSpec((heads, chunk, hd), lambda i: (0, 0, 0)),
            pl.BlockSpec((sdim, D), lambda i: (0, 0)),
            pl.BlockSpec((tm, D), row),
        ],
        out_specs=pl.BlockSpec((tm, D), row),
        out_shape=jax.ShapeDtypeStruct((T, D), F32),
        scratch_shapes=[pltpu.VMEM((tm, sdim), BF16)],
        compiler_params=_params("parallel"),
        name="sgu_outproj_odd",
    )(u, v, sgu_w, sb, w, x)


def _router_kernel(x_ref, g_ref, wr_ref, xn_ref, comb_ref, *, n_experts):
    xn = _rmsnorm_f32(x_ref[...], g_ref[...])
    xn_ref[...] = xn.astype(xn_ref.dtype)
    logits = jnp.dot(xn, wr_ref[...], preferred_element_type=F32, precision=lax.Precision.HIGHEST)
    lane = lax.broadcasted_iota(jnp.int32, logits.shape, 1).astype(F32)
    neg = jnp.float32(-jnp.inf)
    big = jnp.float32(1e9)
    l1 = jnp.where(lane < n_experts, logits, neg)
    m1 = jnp.max(l1, axis=-1, keepdims=True)
    i1 = jnp.min(jnp.where(l1 == m1, lane, big), axis=-1, keepdims=True)
    l2 = jnp.where(lane == i1, neg, l1)
    m2 = jnp.max(l2, axis=-1, keepdims=True)
    i2 = jnp.min(jnp.where(l2 == m2, lane, big), axis=-1, keepdims=True)
    e2 = jnp.exp(m2 - m1)
    den = 1.0 + e2
    comb_ref[...] = jnp.where(lane == i1, 1.0 / den, 0.0) + jnp.where(lane == i2, e2 / den, 0.0)


def _router(x, g, w_router):
    T, D = x.shape
    E = w_router.shape[1]
    wr = jnp.zeros((D, LANES), F32).at[:, :E].set(w_router.astype(F32))
    tm = _pick_tile(T, 512)
    row = lambda i: (i, 0)
    return pl.pallas_call(
        functools.partial(_router_kernel, n_experts=E),
        grid=(T // tm,),
        in_specs=[
            pl.BlockSpec((tm, D), row),
            pl.BlockSpec((1, D), lambda i: (0, 0)),
            pl.BlockSpec((D, LANES), lambda i: (0, 0)),
        ],
        out_specs=[pl.BlockSpec((tm, D), row), pl.BlockSpec((tm, LANES), row)],
        out_shape=[jax.ShapeDtypeStruct((T, D), BF16), jax.ShapeDtypeStruct((T, LANES), F32)],
        compiler_params=_params("parallel"),
        name="moe_router",
    )(x, g.reshape(1, D), wr)


def _moe_dense_kernel(xn_ref, comb_ref, w1_ref, w3_ref, w2_ref, res_ref, lnf_ref, o_ref, acc_ref):
    e = pl.program_id(1)
    f = pl.program_id(2)
    last = (e == pl.num_programs(1) - 1) & (f == pl.num_programs(2) - 1)

    @pl.when((e == 0) & (f == 0))
    def _():
        acc_ref[...] = res_ref[...]

    comb = comb_ref[...]
    lane = lax.broadcasted_iota(jnp.int32, comb.shape, 1)
    gate = jnp.sum(jnp.where(lane == e, comb, 0.0), axis=-1, keepdims=True)
    xn = xn_ref[...]
    a = jnp.dot(xn, w1_ref[...], preferred_element_type=F32)
    b = jnp.dot(xn, w3_ref[...], preferred_element_type=F32)
    act = (a * _sigmoid(a) * b * gate).astype(BF16)
    acc_ref[...] += jnp.dot(act, w2_ref[...], preferred_element_type=F32)

    @pl.when(last)
    def _():
        o_ref[...] = _rmsnorm_f32(acc_ref[...], lnf_ref[...])


def _moe_dense(xn, comb, w1, w3, w2, res, ln_final, *, tm_want=1024, tf_want=1792):
    T, D = xn.shape
    E, _, F = w1.shape
    tm = _pick_tile(T, tm_want)
    tf = tf_want if F % tf_want == 0 else _pick_tile(F, 256)
    row = lambda i, e, f: (i, 0)
    return pl.pallas_call(
        _moe_dense_kernel,
        grid=(T // tm, E, F // tf),
        in_specs=[
            pl.BlockSpec((tm, D), row),
            pl.BlockSpec((tm, LANES), row),
            pl.BlockSpec((None, D, tf), lambda i, e, f: (e, 0, f)),
            pl.BlockSpec((None, D, tf), lambda i, e, f: (e, 0, f)),
            pl.BlockSpec((None, tf, D), lambda i, e, f: (e, f, 0)),
            pl.BlockSpec((tm, D), row),
            pl.BlockSpec((1, D), lambda i, e, f: (0, 0)),
        ],
        out_specs=pl.BlockSpec((tm, D), row),
        out_shape=jax.ShapeDtypeStruct((T, D), F32),
        scratch_shapes=[pltpu.VMEM((tm, D), F32)],
        compiler_params=_params("parallel", "arbitrary", "arbitrary"),
        name="moe_experts",
    )(xn, comb, w1, w3, w2, res, ln_final.reshape(1, D))


def _trunk(x, p):
    B, S, D = x.shape
    T = B * S
    cdim = p["dw_conv_w"].shape[-1]
    ldim = p["rg_lam"].shape[-1]
    assert cdim * 2 == ldim and p["w_in_e"].shape[-1] == 2 * cdim + 2 * ldim

    z = _inproj_e(x, p["ln_mix_e"], p["w_in_e"])
    c = _conv_branch(z, p["dw_conv_w"], p["dw_conv_b"], p["conv_ln_g"], p["conv_ln_b"], B)
    lru = functools.partial(_lru_pass, z, batch=B, col_x=2, col_y=1)
    hb = lru(None, p["rg_conv_w"][1], p["rg_conv_b"][1], p["rg_w_a"][1], p["rg_b_a"][1],
             p["rg_w_i"][1], p["rg_b_i"][1], p["rg_lam"][1], reverse=True)
    r = lru(hb, p["rg_conv_w"][0], p["rg_conv_b"][0], p["rg_w_a"][0], p["rg_b_a"][0],
            p["rg_w_i"][0], p["rg_b_i"][0], p["rg_lam"][0], reverse=False)
    h = _outproj_e(c, r, x, p["w_out_e"]).reshape(T, D)
    h = _ffn(h, p["ln_ffn_e"], p["ffn_w1"], p["ffn_w3"], p["ffn_w2"])

    u, v = _inproj_o(h, p["ln_mix_o"], p["w_in_o"], p["sgu_ln_g"], p["sgu_ln_b"])
    h = _sgu_out(u, v, p["sgu_w"], p["sgu_b"], p["w_out_o"], h)
    xn, comb = _router(h, p["ln_ffn_o"], p["w_router"])
    y = _moe_dense(xn, comb, p["moe_w1"], p["moe_w3"], p["moe_w2"], h, p["ln_final"])
    return y.reshape(B, S, D)


def kernel(x_prompt, x_sample, ln_mix_e, w_in_e, dw_conv_w, dw_conv_b, conv_ln_g, conv_ln_b, rg_conv_w, rg_conv_b, rg_w_a, rg_b_a, rg_w_i, rg_b_i, rg_lam, w_out_e, ln_ffn_e, ffn_w1, ffn_w3, ffn_w2, ln_mix_o, w_in_o, sgu_ln_g, sgu_ln_b, sgu_w, sgu_b, w_out_o, ln_ffn_o, w_router, moe_w1, moe_w3, moe_w2, ln_final):
    assert ln_mix_e.shape[0] == 1 and ln_mix_o.shape[0] == 1, "one even and one odd layer"
    bf = lambda a: a[0].astype(BF16)
    f32 = lambda a: a[0].astype(F32)
    p = {
        "ln_mix_e": f32(ln_mix_e), "w_in_e": bf(w_in_e),
        "dw_conv_w": f32(dw_conv_w), "dw_conv_b": f32(dw_conv_b),
        "conv_ln_g": f32(conv_ln_g), "conv_ln_b": f32(conv_ln_b),
        "rg_conv_w": f32(rg_conv_w), "rg_conv_b": f32(rg_conv_b),
        "rg_w_a": f32(rg_w_a), "rg_b_a": f32(rg_b_a), "rg_w_i": f32(rg_w_i), "rg_b_i": f32(rg_b_i),
        "rg_lam": f32(rg_lam), "w_out_e": bf(w_out_e),
        "ln_ffn_e": f32(ln_ffn_e), "ffn_w1": bf(ffn_w1), "ffn_w3": bf(ffn_w3), "ffn_w2": bf(ffn_w2),
        "ln_mix_o": f32(ln_mix_o), "w_in_o": bf(w_in_o),
        "sgu_ln_g": f32(sgu_ln_g), "sgu_ln_b": f32(sgu_ln_b),
        "sgu_w": bf(sgu_w), "sgu_b": f32(sgu_b), "w_out_o": bf(w_out_o),
        "ln_ffn_o": f32(ln_ffn_o), "w_router": f32(w_router),
        "moe_w1": bf(moe_w1), "moe_w3": bf(moe_w3), "moe_w2": bf(moe_w2),
        "ln_final": ln_final.astype(F32),
    }
    return (_trunk(x_prompt, p), _trunk(x_sample, p))
```

```python
import functools

import jax
import jax.numpy as jnp
from jax import lax
from jax.experimental import pallas as pl
from jax.experimental.pallas import tpu as pltpu

EPS = 1e-6
LRU_C = 8.0
LANES = 128
SUBLANES = 8
MXU_DIM = 256
VMEM_LIMIT_BYTES = 56 * 1024 * 1024
F32 = jnp.float32
BF16 = jnp.bfloat16


def _params(*sem):
    return pltpu.CompilerParams(dimension_semantics=sem, vmem_limit_bytes=VMEM_LIMIT_BYTES)


def _sigmoid(x):
    return 0.5 * (jnp.tanh(0.5 * x) + 1.0)


def _gelu_tanh(x):
    c = 0.7978845608028654
    return 0.5 * x * (1.0 + jnp.tanh(c * (x + 0.044715 * (x * x * x))))


def _rmsnorm_f32(x, g):
    return x * lax.rsqrt(jnp.mean(x * x, axis=-1, keepdims=True) + EPS) * g


def _pick_tile(n, want):
    t = min(n, want)
    while n % t:
        t //= 2
    return t


def _perm_matrix(batch):
    tl = MXU_DIM // batch
    r = jnp.arange(MXU_DIM)
    src = (r % batch) * tl + r // batch
    return (src[:, None] == r[None, :]).astype(BF16)


def _inproj_e_kernel(x_ref, g_ref, p_ref, w_ref, o_ref, xt_ref, *, n_chunk):
    B, tt, _ = x_ref.shape
    tl = MXU_DIM // B
    for q in range(tt // tl):
        xb = jnp.concatenate([x_ref[b, q * tl:(q + 1) * tl, :] for b in range(B)], axis=0)
        xn = _rmsnorm_f32(xb, g_ref[...]).astype(BF16)
        xt_ref[q * MXU_DIM:(q + 1) * MXU_DIM, :] = jnp.dot(
            p_ref[...], xn, preferred_element_type=F32).astype(BF16)
    xt = xt_ref[...]
    n = o_ref.shape[-1]
    for j in range(n // n_chunk):
        cols = slice(j * n_chunk, (j + 1) * n_chunk)
        o_ref[:, cols] = jnp.dot(xt, w_ref[:, cols], preferred_element_type=F32).astype(o_ref.dtype)


def _inproj_e(x, g, w):
    B, S, D = x.shape
    N = w.shape[1]
    assert MXU_DIM % B == 0
    tt = _pick_tile(S, 128)
    assert tt % (MXU_DIM // B) == 0
    return pl.pallas_call(
        functools.partial(_inproj_e_kernel, n_chunk=1024),
        grid=(S // tt,),
        in_specs=[
            pl.BlockSpec((B, tt, D), lambda i: (0, i, 0)),
            pl.BlockSpec((1, D), lambda i: (0, 0)),
            pl.BlockSpec((MXU_DIM, MXU_DIM), lambda i: (0, 0)),
            pl.BlockSpec((D, N), lambda i: (0, 0)),
        ],
        out_specs=pl.BlockSpec((tt * B, N), lambda i: (i, 0)),
        out_shape=jax.ShapeDtypeStruct((S * B, N), BF16),
        scratch_shapes=[pltpu.VMEM((tt * B, D), BF16)],
        compiler_params=_params("parallel"),
        name="inproj_even",
    )(x, g.reshape(1, D), _perm_matrix(B), w)


def _conv_kernel(zp_ref, zm_ref, zn_ref, w_ref, b_ref, g_ref, be_ref, o_ref, s_ref, y_ref,
                 *, rows, halo, cdim, ktaps, batch, unroll, ln_rows):
    i = pl.program_id(0)
    n = pl.num_programs(0)

    def glu(z):
        return z[:, :cdim].astype(F32) * _sigmoid(z[:, cdim:].astype(F32))

    s_ref[halo:halo + rows, :] = glu(zm_ref[...])
    s_ref[0:halo, :] = jnp.where(i > 0, glu(zp_ref[...]), 0.0)
    s_ref[halo + rows:, :] = jnp.where(i < n - 1, glu(zn_ref[...]), 0.0)

    pad = ktaps // 2
    base = halo - pad * batch
    step = batch * unroll
    for j in range(cdim // LANES):
        lanes = slice(j * LANES, (j + 1) * LANES)
        wj = [jnp.broadcast_to(w_ref[k:k + 1, lanes], (batch, LANES)) for k in range(ktaps)]
        bj = jnp.broadcast_to(b_ref[:, lanes], (batch, LANES))

        def body(ci, carry, lanes=lanes, wj=wj, bj=bj):
            r0 = pl.multiple_of(ci * step, step)
            accs = [bj] * unroll
            for m in range(unroll + ktaps - 1):
                xm = s_ref[pl.ds(r0 + base + batch * m, batch), lanes]
                for u in range(unroll):
                    k = m - u
                    if 0 <= k < ktaps:
                        accs[u] = accs[u] + wj[k] * xm
            for u in range(unroll):
                y_ref[pl.ds(r0 + batch * u, batch), lanes] = accs[u]
            return carry

        lax.fori_loop(0, rows // step, body, 0)

    def ln_body(ci, carry):
        r0 = pl.multiple_of(ci * ln_rows, ln_rows)
        y = y_ref[pl.ds(r0, ln_rows), :]
        mu = jnp.mean(y, axis=-1, keepdims=True)
        d = y - mu
        var = jnp.mean(d * d, axis=-1, keepdims=True)
        yn = d * lax.rsqrt(var + EPS) * g_ref[...] + be_ref[...]
        o_ref[pl.ds(r0, ln_rows), :] = (yn * _sigmoid(yn)).astype(o_ref.dtype)
        return carry

    lax.fori_loop(0, rows // ln_rows, ln_body, 0)


def _conv_branch(z, w, b, ln_g, ln_b, batch):
    R = z.shape[0]
    ktaps, cdim = w.shape
    halo = 128
    assert (ktaps // 2) * batch <= halo and batch == SUBLANES
    rows = _pick_tile(R, 2048)
    assert rows % halo == 0
    hb = rows // halo
    nh = R // halo
    vec = lambda a: a.reshape(1, cdim)
    return pl.pallas_call(
        functools.partial(_conv_kernel, rows=rows, halo=halo, cdim=cdim, ktaps=ktaps, batch=batch,
                          unroll=8, ln_rows=min(rows, 256)),
        grid=(R // rows,),
        in_specs=[
            pl.BlockSpec((halo, 2 * cdim), lambda i: (jnp.maximum(i * hb - 1, 0), 0)),
            pl.BlockSpec((rows, 2 * cdim), lambda i: (i, 0)),
            pl.BlockSpec((halo, 2 * cdim), lambda i: (jnp.minimum((i + 1) * hb, nh - 1), 0)),
            pl.BlockSpec((ktaps, cdim), lambda i: (0, 0)),
            pl.BlockSpec((1, cdim), lambda i: (0, 0)),
            pl.BlockSpec((1, cdim), lambda i: (0, 0)),
            pl.BlockSpec((1, cdim), lambda i: (0, 0)),
        ],
        out_specs=pl.BlockSpec((rows, cdim), lambda i: (i, 0)),
        out_shape=jax.ShapeDtypeStruct((R, cdim), BF16),
        scratch_shapes=[pltpu.VMEM((rows + 2 * halo, cdim), F32), pltpu.VMEM((rows, cdim), F32)],
        compiler_params=_params("parallel"),
        name="conv_branch",
    )(z, z, z, w, vec(b), vec(ln_g), vec(ln_b))


def _lru_kernel(*refs, rows, batch, ktaps, sub, reverse, final):
    if final:
        (zx_ref, zy_ref, hb_ref, cw_ref, cb_ref, wg_ref, ba_ref, bi_ref, lam_ref,
         o_ref, xs_ref, a_ref, u_ref, h_ref) = refs
    else:
        (zx_ref, cw_ref, cb_ref, wg_ref, ba_ref, bi_ref, lam_ref,
         o_ref, xs_ref, a_ref, u_ref, h_ref) = refs
    hl = (ktaps - 1) * batch
    main = 0 if reverse else hl
    ldim = zx_ref.shape[-1]
    ngrp = wg_ref.shape[0]
    gw = ldim // ngrp

    @pl.when(pl.program_id(0) == 0)
    def _():
        h_ref[...] = jnp.zeros_like(h_ref)
        xs_ref[...] = jnp.zeros_like(xs_ref)

    xs_ref[main:main + rows, :] = zx_ref[...].astype(F32)

    lam = lam_ref[...]
    sp = jnp.maximum(-lam, 0.0) + jnp.log(1.0 + jnp.exp(-jnp.abs(lam)))
    neg_c_sp = -LRU_C * sp

    def gate_body(ci, carry):
        r0 = pl.multiple_of(ci * sub, sub)
        xc = jnp.broadcast_to(cb_ref[...], (sub, ldim))
        for k in range(ktaps):
            xc = xc + cw_ref[k:k + 1, :] * xs_ref[pl.ds(r0 + batch * k, sub), :]
        xcb = xc.astype(BF16)
        for g in range(ngrp):
            cols = slice(g * gw, (g + 1) * gw)
            gates = jnp.dot(xcb[:, cols], wg_ref[g], preferred_element_type=F32)
            r = _sigmoid(gates[:, :gw] + ba_ref[:, cols])
            ig = _sigmoid(gates[:, gw:] + bi_ref[:, cols])
            log_a = r * neg_c_sp[:, cols]
            a_ref[pl.ds(r0, sub), cols] = jnp.exp(log_a)
            th = jnp.tanh(log_a)
            mult = jnp.sqrt(jnp.maximum(-2.0 * th / (1.0 - th), 1e-12))
            u_ref[pl.ds(r0, sub), cols] = mult * (ig * xc[:, cols])
        return carry

    lax.fori_loop(0, rows // sub, gate_body, 0)

    if reverse:
        xs_ref[rows:rows + hl, :] = xs_ref[0:hl, :]
    else:
        xs_ref[0:hl, :] = xs_ref[rows:rows + hl, :]

    nt = rows // batch

    def scan_body(t, h):
        tt = (nt - 1 - t) if reverse else t
        r = pl.multiple_of(tt * batch, batch)
        h = a_ref[pl.ds(r, batch), :] * h + u_ref[pl.ds(r, batch), :]
        u_ref[pl.ds(r, batch), :] = h
        return h

    h_ref[...] = lax.fori_loop(0, nt, scan_body, h_ref[...], unroll=8)

    def out_body(ci, carry):
        r0 = pl.multiple_of(ci * sub, sub)
        h = u_ref[pl.ds(r0, sub), :]
        if final:
            h = h + hb_ref[pl.ds(r0, sub), :].astype(F32)
            h = _gelu_tanh(zy_ref[pl.ds(r0, sub), :].astype(F32)) * h
        o_ref[pl.ds(r0, sub), :] = h.astype(o_ref.dtype)
        return carry

    lax.fori_loop(0, rows // sub, out_body, 0)


def _pack_blockdiag(w, width):
    H, dh, _ = w.shape
    per = width // dh
    G = H // per
    w = w.reshape(G, per, dh, dh)
    eye = jnp.eye(per, dtype=w.dtype)
    return jnp.einsum("pq,gpij->gpiqj", eye, w).reshape(G, per * dh, per * dh)


def _lru_pass(z, hb, cw, cb, w_a, b_a, w_i, b_i, lam, *, batch, reverse, col_x, col_y):
    R = z.shape[0]
    ktaps, ldim = cw.shape
    final = hb is not None
    rows = _pick_tile(R, 1024)
    nblk = R // rows
    wg = jnp.concatenate([_pack_blockdiag(w_a, MXU_DIM), _pack_blockdiag(w_i, MXU_DIM)], axis=-1).astype(BF16)
    ngrp = wg.shape[0]
    vec = lambda a: a.reshape(1, ldim).astype(F32)
    blk = (lambda i: nblk - 1 - i) if reverse else (lambda i: i)
    const2 = lambda i: (0, 0)
    in_specs = [pl.BlockSpec((rows, ldim), lambda i: (blk(i), col_x))]
    args = [z]
    if final:
        in_specs += [pl.BlockSpec((rows, ldim), lambda i: (blk(i), col_y)),
                     pl.BlockSpec((rows, ldim), lambda i: (blk(i), 0))]
        args += [z, hb]
    in_specs += [
        pl.BlockSpec((ktaps, ldim), const2),
        pl.BlockSpec((1, ldim), const2),
        pl.BlockSpec((ngrp, MXU_DIM, 2 * MXU_DIM), lambda i: (0, 0, 0)),
        pl.BlockSpec((1, ldim), const2),
        pl.BlockSpec((1, ldim), const2),
        pl.BlockSpec((1, ldim), const2),
    ]
    args += [cw, vec(cb), wg, vec(b_a), vec(b_i), vec(lam)]
    hl = (ktaps - 1) * batch
    return pl.pallas_call(
        functools.partial(_lru_kernel, rows=rows, batch=batch, ktaps=ktaps, sub=min(rows, 256),
                          reverse=reverse, final=final),
        grid=(nblk,),
        in_specs=in_specs,
        out_specs=pl.BlockSpec((rows, ldim), lambda i: (blk(i), 0)),
        out_shape=jax.ShapeDtypeStruct((R, ldim), BF16),
        scratch_shapes=[
            pltpu.VMEM((rows + hl, ldim), F32),
            pltpu.VMEM((rows, ldim), F32),
            pltpu.VMEM((rows, ldim), F32),
            pltpu.VMEM((batch, ldim), F32),
        ],
        compiler_params=_params("arbitrary"),
        name="rglru_bwd" if reverse else "rglru_fwd",
    )(*args)


def _outproj_e_kernel(c_ref, r_ref, x_ref, pt_ref, w_ref, o_ref, m_ref):
    B, tt, _ = x_ref.shape
    tl = MXU_DIM // B
    cdim = c_ref.shape[1]
    for q in range(tt // tl):
        rows = slice(q * MXU_DIM, (q + 1) * MXU_DIM)
        m_ref[rows, :cdim] = jnp.dot(pt_ref[...], c_ref[rows, :], preferred_element_type=F32).astype(BF16)
        m_ref[rows, cdim:] = jnp.dot(pt_ref[...], r_ref[rows, :], preferred_element_type=F32).astype(BF16)
    acc = jnp.dot(m_ref[...], w_ref[...], preferred_element_type=F32)
    for q in range(tt // tl):
        for b in range(B):
            r0 = q * MXU_DIM + b * tl
            ts = slice(q * tl, (q + 1) * tl)
            o_ref[b, ts, :] = x_ref[b, ts, :] + acc[r0:r0 + tl, :]


def _outproj_e(c, r, x, w):
    B, S, D = x.shape
    cdim, ldim = c.shape[1], r.shape[1]
    tt = _pick_tile(S, 128)
    return pl.pallas_call(
        _outproj_e_kernel,
        grid=(S // tt,),
        in_specs=[
            pl.BlockSpec((tt * B, cdim), lambda i: (i, 0)),
            pl.BlockSpec((tt * B, ldim), lambda i: (i, 0)),
            pl.BlockSpec((B, tt, D), lambda i: (0, i, 0)),
            pl.BlockSpec((MXU_DIM, MXU_DIM), lambda i: (0, 0)),
            pl.BlockSpec((cdim + ldim, D), lambda i: (0, 0)),
        ],
        out_specs=pl.BlockSpec((B, tt, D), lambda i: (0, i, 0)),
        out_shape=jax.ShapeDtypeStruct((B, S, D), F32),
        scratch_shapes=[pltpu.VMEM((tt * B, cdim + ldim), BF16)],
        compiler_params=_params("parallel"),
        name="outproj_even",
    )(c, r, x, _perm_matrix(B).T, w)


def _ffn_kernel(*refs, starts):
    n_src = len(starts)
    x_refs = refs[:n_src]
    g_ref, w1_ref, w3_ref, w2_ref, o_ref, xn_ref = refs[n_src:]
    i = pl.program_id(0)
    f = pl.program_id(1)
    ends = list(starts[1:]) + [pl.num_programs(0)]
    for x_ref, lo, hi in zip(x_refs, starts, ends):
        @pl.when((f == 0) & (i >= lo) & (i < hi))
        def _(x_ref=x_ref):
            x = x_ref[...]
            xn_ref[...] = _rmsnorm_f32(x, g_ref[...]).astype(BF16)
            o_ref[...] = x

    xn = xn_ref[...]
    a = jnp.dot(xn, w1_ref[...], preferred_element_type=F32)
    b = jnp.dot(xn, w3_ref[...], preferred_element_type=F32)
    act = (a * _sigmoid(a) * b).astype(BF16)
    o_ref[...] += jnp.dot(act, w2_ref[...], preferred_element_type=F32)


def _ffn(xs, g, w1, w3, w2, *, tm=1024, tf_want=1408):
    D = xs[0].shape[1]
    F = w1.shape[1]
    assert all(x.shape[0] % tm == 0 for x in xs)
    counts = [x.shape[0] // tm for x in xs]
    starts = [sum(counts[:k]) for k in range(len(xs))]
    T = tm * sum(counts)
    tf = tf_want if F % tf_want == 0 else _pick_tile(F, 256)

    def src_spec(lo, n):
        return pl.BlockSpec((tm, D), lambda i, f: (jnp.clip(i - lo, 0, n - 1), 0))

    return pl.pallas_call(
        functools.partial(_ffn_kernel, starts=tuple(starts)),
        grid=(T // tm, F // tf),
        in_specs=[src_spec(lo, n) for lo, n in zip(starts, counts)] + [
            pl.BlockSpec((1, D), lambda i, f: (0, 0)),
            pl.BlockSpec((D, tf), lambda i, f: (0, f)),
            pl.BlockSpec((D, tf), lambda i, f: (0, f)),
            pl.BlockSpec((tf, D), lambda i, f: (f, 0)),
        ],
        out_specs=pl.BlockSpec((tm, D), lambda i, f: (i, 0)),
        out_shape=jax.ShapeDtypeStruct((T, D), F32),
        scratch_shapes=[pltpu.VMEM((tm, D), BF16)],
        compiler_params=_params("parallel", "arbitrary"),
        name="ffn_swiglu",
    )(*xs, g.reshape(1, D), w1, w3, w2)


def _inproj_o_kernel(x_ref, g_ref, w_ref, lg_ref, lb_ref, u_ref, v_ref, *, sdim):
    xn = _rmsnorm_f32(x_ref[...], g_ref[...]).astype(BF16)
    zu = _gelu_tanh(jnp.dot(xn, w_ref[:, :sdim], preferred_element_type=F32))
    u_ref[...] = zu.astype(u_ref.dtype)
    zv = _gelu_tanh(jnp.dot(xn, w_ref[:, sdim:], preferred_element_type=F32))
    mu = jnp.mean(zv, axis=-1, keepdims=True)
    d = zv - mu
    var = jnp.mean(d * d, axis=-1, keepdims=True)
    v_ref[...] = (d * lax.rsqrt(var + EPS) * lg_ref[...] + lb_ref[...]).astype(v_ref.dtype)


def _inproj_o(x, g, w, ln_g, ln_b):
    T, D = x.shape
    sdim = w.shape[1] // 2
    tm = _pick_tile(T, 512)
    row = lambda i: (i, 0)
    const = lambda i: (0, 0)
    return pl.pallas_call(
        functools.partial(_inproj_o_kernel, sdim=sdim),
        grid=(T // tm,),
        in_specs=[
            pl.BlockSpec((tm, D), row),
            pl.BlockSpec((1, D), const),
            pl.BlockSpec((D, 2 * sdim), const),
            pl.BlockSpec((1, sdim), const),
            pl.BlockSpec((1, sdim), const),
        ],
        out_specs=[pl.BlockSpec((tm, sdim), row), pl.BlockSpec((tm, sdim), row)],
        out_shape=[jax.ShapeDtypeStruct((T, sdim), BF16), jax.ShapeDtypeStruct((T, sdim), BF16)],
        compiler_params=_params("parallel"),
        name="inproj_odd",
    )(x, g.reshape(1, D), w, ln_g.reshape(1, sdim), ln_b.reshape(1, sdim))


def _sgu_out_kernel(u_ref, v_ref, sw_ref, sb_ref, w_ref, x_ref, o_ref, g_ref, *, chunk, heads):
    tm, sdim = u_ref.shape
    hd = sdim // heads
    for n in range(tm // chunk):
        rows = slice(n * chunk, (n + 1) * chunk)
        for h in range(heads):
            lanes = slice(h * hd, (h + 1) * hd)
            sv = jnp.dot(sw_ref[h], v_ref[rows, lanes], preferred_element_type=F32) + sb_ref[h]
            g_ref[rows, lanes] = (u_ref[rows, lanes].astype(F32) * sv).astype(BF16)
    o_ref[...] = x_ref[...] + jnp.dot(g_ref[...], w_ref[...], preferred_element_type=F32)


def _sgu_out(u, v, sgu_w, sgu_b, w, x):
    T, sdim = u.shape
    D = x.shape[1]
    heads, chunk, _ = sgu_w.shape
    hd = sdim // heads
    tm = _pick_tile(T, 512)
    assert tm % chunk == 0
    sb = jnp.broadcast_to(sgu_b.astype(F32)[:, :, None], (heads, chunk, hd))
    row = lambda i: (i, 0)
    return pl.pallas_call(
        functools.partial(_sgu_out_kernel, chunk=chunk, heads=heads),
        grid=(T // tm,),
        in_specs=[
            pl.BlockSpec((tm, sdim), row),
            pl.BlockSpec((tm, sdim), row),
            pl.BlockSpec((heads, chunk, chunk), lambda i: (0, 0, 0)),
            pl.BlockSpec((heads, chunk, hd), lambda i: (0, 0, 0)),
            pl.BlockSpec((sdim, D), lambda i: (0, 0)),
            pl.BlockSpec((tm, D), row),
        ],
        out_specs=pl.BlockSpec((tm, D), row),
        out_shape=jax.ShapeDtypeStruct((T, D), F32),
        scratch_shapes=[pltpu.VMEM((tm, sdim), BF16)],
        compiler_params=_params("parallel"),
        name="sgu_outproj_odd",
    )(u, v, sgu_w, sb, w, x)


META_E1, META_E2, META_R1, META_R2, META_G1, META_G2 = range(6)
ROUTE_TILE = 512


def _router_kernel(x_ref, g_ref, wr_ref, meta_ref, cnt_ref, run_ref, *, n_experts):
    @pl.when(pl.program_id(0) == 0)
    def _():
        run_ref[...] = jnp.zeros_like(run_ref)

    xn = _rmsnorm_f32(x_ref[...], g_ref[...])
    logits = jnp.dot(xn, wr_ref[...], preferred_element_type=F32, precision=lax.Precision.HIGHEST)
    tm = logits.shape[0]
    lane = lax.broadcasted_iota(jnp.int32, logits.shape, 1).astype(F32)
    neg = jnp.float32(-jnp.inf)
    big = jnp.float32(1e9)
    l1 = jnp.where(lane < n_experts, logits, neg)
    m1 = jnp.max(l1, axis=-1, keepdims=True)
    i1 = jnp.min(jnp.where(l1 == m1, lane, big), axis=-1, keepdims=True)
    l2 = jnp.where(lane == i1, neg, l1)
    m2 = jnp.max(l2, axis=-1, keepdims=True)
    i2 = jnp.min(jnp.where(l2 == m2, lane, big), axis=-1, keepdims=True)
    e2 = jnp.exp(m2 - m1)
    den = 1.0 + e2
    g1 = 1.0 / den
    g2 = e2 / den

    oh1 = jnp.where(lane == i1, 1.0, 0.0)
    oh2 = jnp.where(lane == i2, 1.0, 0.0)
    cnt = oh1 + oh2
    ri = lax.broadcasted_iota(jnp.int32, (tm, tm), 0)
    ci = lax.broadcasted_iota(jnp.int32, (tm, tm), 1)
    lower = jnp.where(ci < ri, 1.0, 0.0).astype(BF16)
    prefix = jnp.dot(lower, cnt.astype(BF16), preferred_element_type=F32) + run_ref[...]
    r1 = jnp.sum(oh1 * prefix, axis=-1, keepdims=True)
    r2 = jnp.sum(oh2 * prefix, axis=-1, keepdims=True)
    run_ref[...] += jnp.sum(cnt, axis=0, keepdims=True)

    meta = jnp.zeros_like(logits)
    for idx, val in ((META_E1, i1), (META_E2, i2), (META_R1, r1), (META_R2, r2), (META_G1, g1), (META_G2, g2)):
        meta = jnp.where(lane == idx, val, meta)
    meta_ref[...] = meta
    cnt_ref[...] = jnp.broadcast_to(run_ref[...], cnt_ref.shape)


def _router(x, g, w_router):
    T, D = x.shape
    E = w_router.shape[1]
    wr = jnp.zeros((D, LANES), F32).at[:, :E].set(w_router.astype(F32))
    tm = ROUTE_TILE
    assert T % tm == 0
    row = lambda i: (i, 0)
    return pl.pallas_call(
        functools.partial(_router_kernel, n_experts=E),
        grid=(T // tm,),
        in_specs=[
            pl.BlockSpec((tm, D), row),
            pl.BlockSpec((1, D), lambda i: (0, 0)),
            pl.BlockSpec((D, LANES), lambda i: (0, 0)),
        ],
        out_specs=[pl.BlockSpec((tm, LANES), row), pl.BlockSpec((SUBLANES, LANES), lambda i: (0, 0))],
        out_shape=[jax.ShapeDtypeStruct((T, LANES), F32), jax.ShapeDtypeStruct((SUBLANES, LANES), F32)],
        scratch_shapes=[pltpu.VMEM((1, LANES), F32)],
        compiler_params=_params("arbitrary"),
        name="moe_router",
    )(x, g.reshape(1, D), wr)


def _route_plan(meta, cnt, n_experts, tmx, n_tiles):
    counts = cnt[0, :n_experts].astype(jnp.int32)
    nt = (counts + tmx - 1) // tmx
    tend = jnp.cumsum(nt)
    off = (tend - nt) * tmx
    eids = jnp.arange(n_experts, dtype=jnp.int32)

    def pos(e_lane, r_lane):
        e = meta[:, e_lane].astype(jnp.int32)
        base = jnp.sum(jnp.where(e[:, None] == eids[None, :], off[None, :], 0), axis=1)
        return base + meta[:, r_lane].astype(jnp.int32)

    T = meta.shape[0]
    tm = ROUTE_TILE
    p1 = pos(META_E1, META_R1).reshape(T // tm, 1, tm)
    p2 = pos(META_E2, META_R2).reshape(T // tm, 1, tm)
    tiles = jnp.arange(n_tiles, dtype=jnp.int32)
    tile_expert = jnp.minimum(jnp.sum((tend[None, :] <= tiles[:, None]).astype(jnp.int32), axis=1), n_experts - 1)
    return jnp.concatenate([p1, p2], axis=2), tile_expert.astype(jnp.int32), tend[-1:].astype(jnp.int32)


def _dispatch_kernel(pos_ref, x_ref, g_ref, xs_in_ref, xs_ref, xn_ref, sem):
    del xs_in_ref
    tm = x_ref.shape[0]
    xn_ref[...] = _rmsnorm_f32(x_ref[...], g_ref[...])

    def row_copy(r, p):
        return pltpu.make_async_copy(xn_ref.at[pl.ds(r, 1)], xs_ref.at[pl.ds(p, 1)], sem)

    def issue(r, carry):
        row_copy(r, pos_ref[0, r]).start()
        row_copy(r, pos_ref[0, tm + r]).start()
        return carry

    lax.fori_loop(0, tm, issue, 0)
    for _ in range(2):
        pltpu.make_async_copy(xn_ref, xs_ref.at[pl.ds(0, tm)], sem).wait()


def _dispatch(x, g, pos, n_rows):
    T, D = x.shape
    tm = ROUTE_TILE
    xs0 = jnp.zeros((n_rows, D), F32)
    return pl.pallas_call(
        _dispatch_kernel,
        grid=(T // tm,),
        in_specs=[
            pl.BlockSpec((None, 1, 2 * tm), lambda i: (i, 0, 0), memory_space=pltpu.SMEM),
            pl.BlockSpec((tm, D), lambda i: (i, 0)),
            pl.BlockSpec((1, D), lambda i: (0, 0)),
            pl.BlockSpec(memory_space=pl.ANY),
        ],
        out_specs=pl.BlockSpec(memory_space=pl.ANY),
        out_shape=jax.ShapeDtypeStruct((n_rows, D), F32),
        scratch_shapes=[pltpu.VMEM((tm, D), F32), pltpu.SemaphoreType.DMA(())],
        input_output_aliases={3: 0},
        compiler_params=_params("arbitrary"),
        name="moe_dispatch",
    )(pos, x, g.reshape(1, D), xs0)


def _moe_experts_kernel(te_ref, na_ref, x_ref, w1_ref, w3_ref, w2_ref, o_ref, xb_ref):
    del te_ref
    f = pl.program_id(1)

    @pl.when(f == 0)
    def _():
        o_ref[...] = jnp.zeros_like(o_ref)
        xb_ref[...] = x_ref[...].astype(BF16)

    @pl.when(pl.program_id(0) < na_ref[0])
    def _():
        xn = xb_ref[...]
        a = jnp.dot(xn, w1_ref[...], preferred_element_type=F32)
        b = jnp.dot(xn, w3_ref[...], preferred_element_type=F32)
        act = (a * _sigmoid(a) * b).astype(BF16)
        o_ref[...] += jnp.dot(act, w2_ref[...], preferred_element_type=F32)


def _moe_experts(xs, tile_expert, n_active, w1, w3, w2, *, tmx, tf):
    n_rows, D = xs.shape
    E, _, F = w1.shape
    assert F % tf == 0 and n_rows % tmx == 0
    nf = F // tf
    fsel = lambda i, f, na: jnp.where(i < na[0], f, nf - 1)
    grid_spec = pltpu.PrefetchScalarGridSpec(
        num_scalar_prefetch=2,
        grid=(n_rows // tmx, nf),
        in_specs=[
            pl.BlockSpec((tmx, D), lambda i, f, te, na: (i, 0)),
            pl.BlockSpec((None, D, tf), lambda i, f, te, na: (te[i], 0, fsel(i, f, na))),
            pl.BlockSpec((None, D, tf), lambda i, f, te, na: (te[i], 0, fsel(i, f, na))),
            pl.BlockSpec((None, tf, D), lambda i, f, te, na: (te[i], fsel(i, f, na), 0)),
        ],
        out_specs=pl.BlockSpec((tmx, D), lambda i, f, te, na: (i, 0)),
        scratch_shapes=[pltpu.VMEM((tmx, D), BF16)],
    )
    return pl.pallas_call(
        _moe_experts_kernel,
        grid_spec=grid_spec,
        out_shape=jax.ShapeDtypeStruct((n_rows, D), F32),
        compiler_params=_params("parallel", "arbitrary"),
        name="moe_experts",
    )(tile_expert, n_active, xs, w1, w3, w2)


def _combine_kernel(pos_ref, meta_ref, res_ref, lnf_ref, y_ref, o_ref, ya_ref, yb_ref, sem):
    tm = res_ref.shape[0]

    def issue(r, carry):
        pltpu.make_async_copy(y_ref.at[pl.ds(pos_ref[0, r], 1)], ya_ref.at[pl.ds(r, 1)], sem).start()
        pltpu.make_async_copy(y_ref.at[pl.ds(pos_ref[0, tm + r], 1)], yb_ref.at[pl.ds(r, 1)], sem).start()
        return carry

    lax.fori_loop(0, tm, issue, 0)
    pltpu.make_async_copy(y_ref.at[pl.ds(0, tm)], ya_ref, sem).wait()
    pltpu.make_async_copy(y_ref.at[pl.ds(0, tm)], yb_ref, sem).wait()

    meta = meta_ref[...]
    g1 = meta[:, META_G1:META_G1 + 1]
    g2 = meta[:, META_G2:META_G2 + 1]
    h = res_ref[...] + g1 * ya_ref[...] + g2 * yb_ref[...]
    o_ref[...] = _rmsnorm_f32(h, lnf_ref[...])


def _combine(y, pos, meta, res, ln_final, t_len, t_off):
    D = res.shape[1]
    tm = ROUTE_TILE
    assert t_len % tm == 0 and t_off % tm == 0
    off = t_off // tm
    return pl.pallas_call(
        _combine_kernel,
        grid=(t_len // tm,),
        in_specs=[
            pl.BlockSpec((None, 1, 2 * tm), lambda i: (off + i, 0, 0), memory_space=pltpu.SMEM),
            pl.BlockSpec((tm, LANES), lambda i: (off + i, 0)),
            pl.BlockSpec((tm, D), lambda i: (off + i, 0)),
            pl.BlockSpec((1, D), lambda i: (0, 0)),
            pl.BlockSpec(memory_space=pl.ANY),
        ],
        out_specs=pl.BlockSpec((tm, D), lambda i: (i, 0)),
        out_shape=jax.ShapeDtypeStruct((t_len, D), F32),
        scratch_shapes=[pltpu.VMEM((tm, D), F32), pltpu.VMEM((tm, D), F32), pltpu.SemaphoreType.DMA(())],
        compiler_params=_params("arbitrary"),
        name="moe_combine",
    )(pos, meta, res, ln_final.reshape(1, D), y)


MOE_ROW_TILE = 1024
MOE_FF_TILE = 896


def _even_mixer(x, p):
    B = x.shape[0]
    cdim = p["dw_conv_w"].shape[-1]
    ldim = p["rg_lam"].shape[-1]
    assert cdim * 2 == ldim and p["w_in_e"].shape[-1] == 2 * cdim + 2 * ldim
    z = _inproj_e(x, p["ln_mix_e"], p["w_in_e"])
    c = _conv_branch(z, p["dw_conv_w"], p["dw_conv_b"], p["conv_ln_g"], p["conv_ln_b"], B)
    lru = functools.partial(_lru_pass, z, batch=B, col_x=2, col_y=1)
    hb = lru(None, p["rg_conv_w"][1], p["rg_conv_b"][1], p["rg_w_a"][1], p["rg_b_a"][1],
             p["rg_w_i"][1], p["rg_b_i"][1], p["rg_lam"][1], reverse=True)
    r = lru(hb, p["rg_conv_w"][0], p["rg_conv_b"][0], p["rg_w_a"][0], p["rg_b_a"][0],
            p["rg_w_i"][0], p["rg_b_i"][0], p["rg_lam"][0], reverse=False)
    return _outproj_e(c, r, x, p["w_out_e"])


def _trunks(xs, p):
    D = xs[0].shape[-1]
    lens = [x.shape[0] * x.shape[1] for x in xs]
    offs = [sum(lens[:k]) for k in range(len(xs))]
    T = sum(lens)

    h = _ffn([_even_mixer(x, p).reshape(n, D) for x, n in zip(xs, lens)],
             p["ln_ffn_e"], p["ffn_w1"], p["ffn_w3"], p["ffn_w2"])

    u, v = _inproj_o(h, p["ln_mix_o"], p["w_in_o"], p["sgu_ln_g"], p["sgu_ln_b"])
    h = _sgu_out(u, v, p["sgu_w"], p["sgu_b"], p["w_out_o"], h)

    E = p["w_router"].shape[1]
    tmx = MOE_ROW_TILE
    tf = MOE_FF_TILE if p["moe_w1"].shape[-1] % MOE_FF_TILE == 0 else _pick_tile(p["moe_w1"].shape[-1], 512)
    n_tiles = (2 * T) // tmx + E
    meta, cnt = _router(h, p["ln_ffn_o"], p["w_router"])
    pos, tile_expert, n_active = _route_plan(meta, cnt, E, tmx, n_tiles)
    xsort = _dispatch(h, p["ln_ffn_o"], pos, n_tiles * tmx)
    y = _moe_experts(xsort, tile_expert, n_active, p["moe_w1"], p["moe_w3"], p["moe_w2"], tmx=tmx, tf=tf)
    return tuple(_combine(y, pos, meta, h, p["ln_final"], n, off).reshape(x.shape)
                 for x, n, off in zip(xs, lens, offs))


def kernel(x_prompt, x_sample, ln_mix_e, w_in_e, dw_conv_w, dw_conv_b, conv_ln_g, conv_ln_b, rg_conv_w, rg_conv_b, rg_w_a, rg_b_a, rg_w_i, rg_b_i, rg_lam, w_out_e, ln_ffn_e, ffn_w1, ffn_w3, ffn_w2, ln_mix_o, w_in_o, sgu_ln_g, sgu_ln_b, sgu_w, sgu_b, w_out_o, ln_ffn_o, w_router, moe_w1, moe_w3, moe_w2, ln_final):
    assert ln_mix_e.shape[0] == 1 and ln_mix_o.shape[0] == 1, "one even and one odd layer"
    bf = lambda a: a[0].astype(BF16)
    f32 = lambda a: a[0].astype(F32)
    p = {
        "ln_mix_e": f32(ln_mix_e), "w_in_e": bf(w_in_e),
        "dw_conv_w": f32(dw_conv_w), "dw_conv_b": f32(dw_conv_b),
        "conv_ln_g": f32(conv_ln_g), "conv_ln_b": f32(conv_ln_b),
        "rg_conv_w": f32(rg_conv_w), "rg_conv_b": f32(rg_conv_b),
        "rg_w_a": f32(rg_w_a), "rg_b_a": f32(rg_b_a), "rg_w_i": f32(rg_w_i), "rg_b_i": f32(rg_b_i),
        "rg_lam": f32(rg_lam), "w_out_e": bf(w_out_e),
        "ln_ffn_e": f32(ln_ffn_e), "ffn_w1": bf(ffn_w1), "ffn_w3": bf(ffn_w3), "ffn_w2": bf(ffn_w2),
        "ln_mix_o": f32(ln_mix_o), "w_in_o": bf(w_in_o),
        "sgu_ln_g": f32(sgu_ln_g), "sgu_ln_b": f32(sgu_ln_b),
        "sgu_w": bf(sgu_w), "sgu_b": f32(sgu_b), "w_out_o": bf(w_out_o),
        "ln_ffn_o": f32(ln_ffn_o), "w_router": f32(w_router),
        "moe_w1": bf(moe_w1), "moe_w3": bf(moe_w3), "moe_w2": bf(moe_w2),
        "ln_final": ln_final.astype(F32),
    }
    return _trunks([x_prompt, x_sample], p)
```

```python
import functools

import jax
import jax.numpy as jnp
from jax import lax
from jax.experimental import pallas as pl
from jax.experimental.pallas import tpu as pltpu

EPS = 1e-6
LRU_C = 8.0
LANES = 128
SUBLANES = 8
MXU_DIM = 256
VMEM_LIMIT_BYTES = 56 * 1024 * 1024
F32 = jnp.float32
BF16 = jnp.bfloat16


def _params(*sem):
    return pltpu.CompilerParams(dimension_semantics=sem, vmem_limit_bytes=VMEM_LIMIT_BYTES)


def _sigmoid(x):
    return 0.5 * (jnp.tanh(0.5 * x) + 1.0)


def _gelu_tanh(x):
    c = 0.7978845608028654
    return 0.5 * x * (1.0 + jnp.tanh(c * (x + 0.044715 * (x * x * x))))


def _rmsnorm_f32(x, g):
    return x * lax.rsqrt(jnp.mean(x * x, axis=-1, keepdims=True) + EPS) * g


def _pick_tile(n, want):
    t = min(n, want)
    while n % t:
        t //= 2
    return t


def _perm_matrix(batch):
    tl = MXU_DIM // batch
    r = jnp.arange(MXU_DIM)
    src = (r % batch) * tl + r // batch
    return (src[:, None] == r[None, :]).astype(BF16)


def _inproj_e_kernel(x_ref, g_ref, p_ref, w_ref, o_ref, xt_ref, *, n_chunk):
    B, tt, _ = x_ref.shape
    tl = MXU_DIM // B
    for q in range(tt // tl):
        xb = jnp.concatenate([x_ref[b, q * tl:(q + 1) * tl, :] for b in range(B)], axis=0)
        xn = _rmsnorm_f32(xb, g_ref[...]).astype(BF16)
        xt_ref[q * MXU_DIM:(q + 1) * MXU_DIM, :] = jnp.dot(
            p_ref[...], xn, preferred_element_type=F32).astype(BF16)
    xt = xt_ref[...]
    n = o_ref.shape[-1]
    for j in range(n // n_chunk):
        cols = slice(j * n_chunk, (j + 1) * n_chunk)
        o_ref[:, cols] = jnp.dot(xt, w_ref[:, cols], preferred_element_type=F32).astype(o_ref.dtype)


def _inproj_e(x, g, w):
    B, S, D = x.shape
    N = w.shape[1]
    assert MXU_DIM % B == 0
    tt = _pick_tile(S, 128)
    assert tt % (MXU_DIM // B) == 0
    return pl.pallas_call(
        functools.partial(_inproj_e_kernel, n_chunk=1024),
        grid=(S // tt,),
        in_specs=[
            pl.BlockSpec((B, tt, D), lambda i: (0, i, 0)),
            pl.BlockSpec((1, D), lambda i: (0, 0)),
            pl.BlockSpec((MXU_DIM, MXU_DIM), lambda i: (0, 0)),
            pl.BlockSpec((D, N), lambda i: (0, 0)),
        ],
        out_specs=pl.BlockSpec((tt * B, N), lambda i: (i, 0)),
        out_shape=jax.ShapeDtypeStruct((S * B, N), BF16),
        scratch_shapes=[pltpu.VMEM((tt * B, D), BF16)],
        compiler_params=_params("parallel"),
        name="inproj_even",
    )(x, g.reshape(1, D), _perm_matrix(B), w)


def _conv_kernel(zp_ref, zm_ref, zn_ref, w_ref, b_ref, g_ref, be_ref, o_ref, s_ref, y_ref,
                 *, rows, halo, cdim, ktaps, batch, unroll, ln_rows, n_steps):
    i = pl.program_id(0)
    n = n_steps

    def glu(z):
        return z[:, :cdim].astype(F32) * _sigmoid(z[:, cdim:].astype(F32))

    s_ref[halo:halo + rows, :] = glu(zm_ref[...])
    s_ref[0:halo, :] = jnp.where(i > 0, glu(zp_ref[...]), 0.0)
    s_ref[halo + rows:, :] = jnp.where(i < n - 1, glu(zn_ref[...]), 0.0)

    pad = ktaps // 2
    base = halo - pad * batch
    step = batch * unroll
    for j in range(cdim // LANES):
        lanes = slice(j * LANES, (j + 1) * LANES)
        wj = [jnp.broadcast_to(w_ref[k:k + 1, lanes], (batch, LANES)) for k in range(ktaps)]
        bj = jnp.broadcast_to(b_ref[:, lanes], (batch, LANES))

        def body(ci, carry, lanes=lanes, wj=wj, bj=bj):
            r0 = pl.multiple_of(ci * step, step)
            accs = [bj] * unroll
            for m in range(unroll + ktaps - 1):
                xm = s_ref[pl.ds(r0 + base + batch * m, batch), lanes]
                for u in range(unroll):
                    k = m - u
                    if 0 <= k < ktaps:
                        accs[u] = accs[u] + wj[k] * xm
            for u in range(unroll):
                y_ref[pl.ds(r0 + batch * u, batch), lanes] = accs[u]
            return carry

        lax.fori_loop(0, rows // step, body, 0)

    def ln_body(ci, carry):
        r0 = pl.multiple_of(ci * ln_rows, ln_rows)
        y = y_ref[pl.ds(r0, ln_rows), :]
        mu = jnp.mean(y, axis=-1, keepdims=True)
        d = y - mu
        var = jnp.mean(d * d, axis=-1, keepdims=True)
        yn = d * lax.rsqrt(var + EPS) * g_ref[...] + be_ref[...]
        o_ref[pl.ds(r0, ln_rows), :] = (yn * _sigmoid(yn)).astype(o_ref.dtype)
        return carry

    lax.fori_loop(0, rows // ln_rows, ln_body, 0)


def _conv_branch(z, w, b, ln_g, ln_b, batch):
    R = z.shape[0]
    ktaps, cdim = w.shape
    halo = 128
    assert (ktaps // 2) * batch <= halo and batch == SUBLANES
    rows = _pick_tile(R, 2048)
    assert rows % halo == 0
    hb = rows // halo
    nh = R // halo
    vec = lambda a: a.reshape(1, cdim)
    return pl.pallas_call(
        functools.partial(_conv_kernel, rows=rows, halo=halo, cdim=cdim, ktaps=ktaps, batch=batch,
                          unroll=8, ln_rows=min(rows, 256), n_steps=R // rows),
        grid=(R // rows,),
        in_specs=[
            pl.BlockSpec((halo, 2 * cdim), lambda i: (jnp.maximum(i * hb - 1, 0), 0)),
            pl.BlockSpec((rows, 2 * cdim), lambda i: (i, 0)),
            pl.BlockSpec((halo, 2 * cdim), lambda i: (jnp.minimum((i + 1) * hb, nh - 1), 0)),
            pl.BlockSpec((ktaps, cdim), lambda i: (0, 0)),
            pl.BlockSpec((1, cdim), lambda i: (0, 0)),
            pl.BlockSpec((1, cdim), lambda i: (0, 0)),
            pl.BlockSpec((1, cdim), lambda i: (0, 0)),
        ],
        out_specs=pl.BlockSpec((rows, cdim), lambda i: (i, 0)),
        out_shape=jax.ShapeDtypeStruct((R, cdim), BF16),
        scratch_shapes=[pltpu.VMEM((rows + 2 * halo, cdim), F32), pltpu.VMEM((rows, cdim), F32)],
        compiler_params=_params("parallel"),
        name="conv_branch",
    )(z, z, z, w, vec(b), vec(ln_g), vec(ln_b))


def _lru_kernel(*refs, rows, batch, ktaps, sub, reverse, final):
    if final:
        (zx_ref, zy_ref, hb_ref, cw_ref, cb_ref, wg_ref, ba_ref, bi_ref, lam_ref,
         o_ref, xs_ref, a_ref, u_ref, h_ref) = refs
    else:
        (zx_ref, cw_ref, cb_ref, wg_ref, ba_ref, bi_ref, lam_ref,
         o_ref, xs_ref, a_ref, u_ref, h_ref) = refs
    hl = (ktaps - 1) * batch
    main = 0 if reverse else hl
    ldim = zx_ref.shape[-1]
    ngrp = wg_ref.shape[0]
    gw = ldim // ngrp

    @pl.when(pl.program_id(0) == 0)
    def _():
        h_ref[...] = jnp.zeros_like(h_ref)
        xs_ref[...] = jnp.zeros_like(xs_ref)

    xs_ref[main:main + rows, :] = zx_ref[...].astype(F32)

    lam = lam_ref[...]
    sp = jnp.maximum(-lam, 0.0) + jnp.log(1.0 + jnp.exp(-jnp.abs(lam)))
    half_neg_c_sp = -0.5 * LRU_C * sp

    def gate_body(ci, carry):
        r0 = pl.multiple_of(ci * sub, sub)
        xc = jnp.broadcast_to(cb_ref[...], (sub, ldim))
        for k in range(ktaps):
            xc = xc + cw_ref[k:k + 1, :] * xs_ref[pl.ds(r0 + batch * k, sub), :]
        xcb = xc.astype(BF16)
        for g in range(ngrp):
            cols = slice(g * gw, (g + 1) * gw)
            gates = jnp.dot(xcb[:, cols], wg_ref[g], preferred_element_type=F32)
            tr = jnp.tanh(gates[:, :gw] + ba_ref[:, cols])
            ig = 0.5 * (jnp.tanh(gates[:, gw:] + bi_ref[:, cols]) + 1.0)
            log_a = (tr + 1.0) * half_neg_c_sp[:, cols]
            a_ref[pl.ds(r0, sub), cols] = jnp.exp(log_a)
            th = jnp.tanh(log_a)
            mult = jnp.sqrt(jnp.maximum(-2.0 * th / (1.0 - th), 1e-12))
            u_ref[pl.ds(r0, sub), cols] = mult * (ig * xc[:, cols])
        return carry

    lax.fori_loop(0, rows // sub, gate_body, 0)

    if reverse:
        xs_ref[rows:rows + hl, :] = xs_ref[0:hl, :]
    else:
        xs_ref[0:hl, :] = xs_ref[rows:rows + hl, :]

    nt = rows // batch

    def scan_body(t, h):
        tt = (nt - 1 - t) if reverse else t
        r = pl.multiple_of(tt * batch, batch)
        h = a_ref[pl.ds(r, batch), :] * h + u_ref[pl.ds(r, batch), :]
        u_ref[pl.ds(r, batch), :] = h
        return h

    h_ref[...] = lax.fori_loop(0, nt, scan_body, h_ref[...], unroll=8)

    def out_body(ci, carry):
        r0 = pl.multiple_of(ci * sub, sub)
        h = u_ref[pl.ds(r0, sub), :]
        if final:
            h = h + hb_ref[pl.ds(r0, sub), :].astype(F32)
            h = _gelu_tanh(zy_ref[pl.ds(r0, sub), :].astype(F32)) * h
        o_ref[pl.ds(r0, sub), :] = h.astype(o_ref.dtype)
        return carry

    lax.fori_loop(0, rows // sub, out_body, 0)


def _pack_blockdiag(w, width):
    H, dh, _ = w.shape
    per = width // dh
    G = H // per
    w = w.reshape(G, per, dh, dh)
    eye = jnp.eye(per, dtype=w.dtype)
    return jnp.einsum("pq,gpij->gpiqj", eye, w).reshape(G, per * dh, per * dh)


def _lru_pass(z, hb, cw, cb, w_a, b_a, w_i, b_i, lam, *, batch, reverse, col_x, col_y):
    R = z.shape[0]
    ktaps, ldim = cw.shape
    final = hb is not None
    rows = _pick_tile(R, 1024)
    nblk = R // rows
    wg = (0.5 * jnp.concatenate([_pack_blockdiag(w_a, MXU_DIM), _pack_blockdiag(w_i, MXU_DIM)], axis=-1)).astype(BF16)
    b_a, b_i = 0.5 * b_a, 0.5 * b_i
    ngrp = wg.shape[0]
    vec = lambda a: a.reshape(1, ldim).astype(F32)
    blk = (lambda i: nblk - 1 - i) if reverse else (lambda i: i)
    const2 = lambda i: (0, 0)
    in_specs = [pl.BlockSpec((rows, ldim), lambda i: (blk(i), col_x))]
    args = [z]
    if final:
        in_specs += [pl.BlockSpec((rows, ldim), lambda i: (blk(i), col_y)),
                     pl.BlockSpec((rows, ldim), lambda i: (blk(i), 0))]
        args += [z, hb]
    in_specs += [
        pl.BlockSpec((ktaps, ldim), const2),
        pl.BlockSpec((1, ldim), const2),
        pl.BlockSpec((ngrp, MXU_DIM, 2 * MXU_DIM), lambda i: (0, 0, 0)),
        pl.BlockSpec((1, ldim), const2),
        pl.BlockSpec((1, ldim), const2),
        pl.BlockSpec((1, ldim), const2),
    ]
    args += [cw, vec(cb), wg, vec(b_a), vec(b_i), vec(lam)]
    hl = (ktaps - 1) * batch
    return pl.pallas_call(
        functools.partial(_lru_kernel, rows=rows, batch=batch, ktaps=ktaps, sub=min(rows, 256),
                          reverse=reverse, final=final),
        grid=(nblk,),
        in_specs=in_specs,
        out_specs=pl.BlockSpec((rows, ldim), lambda i: (blk(i), 0)),
        out_shape=jax.ShapeDtypeStruct((R, ldim), BF16),
        scratch_shapes=[
            pltpu.VMEM((rows + hl, ldim), F32),
            pltpu.VMEM((rows, ldim), F32),
            pltpu.VMEM((rows, ldim), F32),
            pltpu.VMEM((batch, ldim), F32),
        ],
        compiler_params=_params("arbitrary"),
        name="rglru_bwd" if reverse else "rglru_fwd",
    )(*args)


def _outproj_e_kernel(c_ref, r_ref, x_ref, pt_ref, w_ref, o_ref, m_ref):
    B, tt, _ = x_ref.shape
    tl = MXU_DIM // B
    cdim = c_ref.shape[1]
    for q in range(tt // tl):
        rows = slice(q * MXU_DIM, (q + 1) * MXU_DIM)
        m_ref[rows, :cdim] = jnp.dot(pt_ref[...], c_ref[rows, :], preferred_element_type=F32).astype(BF16)
        m_ref[rows, cdim:] = jnp.dot(pt_ref[...], r_ref[rows, :], preferred_element_type=F32).astype(BF16)
    acc = jnp.dot(m_ref[...], w_ref[...], preferred_element_type=F32)
    for q in range(tt // tl):
        for b in range(B):
            r0 = q * MXU_DIM + b * tl
            ts = slice(q * tl, (q + 1) * tl)
            o_ref[b, ts, :] = x_ref[b, ts, :] + acc[r0:r0 + tl, :]


def _outproj_e(c, r, x, w):
    B, S, D = x.shape
    cdim, ldim = c.shape[1], r.shape[1]
    tt = _pick_tile(S, 128)
    return pl.pallas_call(
        _outproj_e_kernel,
        grid=(S // tt,),
        in_specs=[
            pl.BlockSpec((tt * B, cdim), lambda i: (i, 0)),
            pl.BlockSpec((tt * B, ldim), lambda i: (i, 0)),
            pl.BlockSpec((B, tt, D), lambda i: (0, i, 0)),
            pl.BlockSpec((MXU_DIM, MXU_DIM), lambda i: (0, 0)),
            pl.BlockSpec((cdim + ldim, D), lambda i: (0, 0)),
        ],
        out_specs=pl.BlockSpec((B, tt, D), lambda i: (0, i, 0)),
        out_shape=jax.ShapeDtypeStruct((B, S, D), F32),
        scratch_shapes=[pltpu.VMEM((tt * B, cdim + ldim), BF16)],
        compiler_params=_params("parallel"),
        name="outproj_even",
    )(c, r, x, _perm_matrix(B).T, w)


def _ffn_kernel(*refs, bounds):
    n_src = len(bounds) - 1
    x_refs = refs[:n_src]
    g_ref, w1_ref, w3_ref, w2_ref, o_ref, xn_ref = refs[n_src:]
    i = pl.program_id(0)
    f = pl.program_id(1)
    for x_ref, lo, hi in zip(x_refs, bounds[:-1], bounds[1:]):
        @pl.when((f == 0) & (i >= lo) & (i < hi))
        def _(x_ref=x_ref):
            x = x_ref[...]
            xn_ref[...] = _rmsnorm_f32(x, g_ref[...]).astype(BF16)
            o_ref[...] = x

    xn = xn_ref[...]
    a = jnp.dot(xn, w1_ref[...], preferred_element_type=F32)
    b = jnp.dot(xn, w3_ref[...], preferred_element_type=F32)
    act = (a * _sigmoid(a) * b).astype(BF16)
    o_ref[...] += jnp.dot(act, w2_ref[...], preferred_element_type=F32)


def _ffn(xs, g, w1, w3, w2, *, tm=1024, tf_want=1408):
    D = xs[0].shape[1]
    F = w1.shape[1]
    assert all(x.shape[0] % tm == 0 for x in xs)
    counts = [x.shape[0] // tm for x in xs]
    starts = [sum(counts[:k]) for k in range(len(xs))]
    T = tm * sum(counts)
    tf = tf_want if F % tf_want == 0 else _pick_tile(F, 256)

    def src_spec(lo, n):
        return pl.BlockSpec((tm, D), lambda i, f: (jnp.clip(i - lo, 0, n - 1), 0))

    return pl.pallas_call(
        functools.partial(_ffn_kernel, bounds=tuple(starts) + (T // tm,)),
        grid=(T // tm, F // tf),
        in_specs=[src_spec(lo, n) for lo, n in zip(starts, counts)] + [
            pl.BlockSpec((1, D), lambda i, f: (0, 0)),
            pl.BlockSpec((D, tf), lambda i, f: (0, f)),
            pl.BlockSpec((D, tf), lambda i, f: (0, f)),
            pl.BlockSpec((tf, D), lambda i, f: (f, 0)),
        ],
        out_specs=pl.BlockSpec((tm, D), lambda i, f: (i, 0)),
        out_shape=jax.ShapeDtypeStruct((T, D), F32),
        scratch_shapes=[pltpu.VMEM((tm, D), BF16)],
        compiler_params=_params("parallel", "arbitrary"),
        name="ffn_swiglu",
    )(*xs, g.reshape(1, D), w1, w3, w2)


def _inproj_o_kernel(x_ref, g_ref, w_ref, lg_ref, lb_ref, u_ref, v_ref, zv_ref, *, sdim, n_chunk):
    xn = _rmsnorm_f32(x_ref[...], g_ref[...]).astype(BF16)
    for j in range(sdim // n_chunk):
        cols = slice(j * n_chunk, (j + 1) * n_chunk)
        zu = _gelu_tanh(jnp.dot(xn, w_ref[:, cols], preferred_element_type=F32))
        u_ref[:, cols] = zu.astype(u_ref.dtype)
    for j in range(sdim // n_chunk):
        cols = slice(j * n_chunk, (j + 1) * n_chunk)
        wcols = slice(sdim + j * n_chunk, sdim + (j + 1) * n_chunk)
        zv_ref[:, cols] = _gelu_tanh(jnp.dot(xn, w_ref[:, wcols], preferred_element_type=F32))
    zv = zv_ref[...]
    mu = jnp.mean(zv, axis=-1, keepdims=True)
    d = zv - mu
    var = jnp.mean(d * d, axis=-1, keepdims=True)
    v_ref[...] = (d * lax.rsqrt(var + EPS) * lg_ref[...] + lb_ref[...]).astype(v_ref.dtype)


def _inproj_o(x, g, w, ln_g, ln_b):
    T, D = x.shape
    sdim = w.shape[1] // 2
    tm = _pick_tile(T, 512)
    row = lambda i: (i, 0)
    const = lambda i: (0, 0)
    return pl.pallas_call(
        functools.partial(_inproj_o_kernel, sdim=sdim, n_chunk=_pick_tile(sdim, 512)),
        grid=(T // tm,),
        scratch_shapes=[pltpu.VMEM((tm, sdim), F32)],
        in_specs=[
            pl.BlockSpec((tm, D), row),
            pl.BlockSpec((1, D), const),
            pl.BlockSpec((D, 2 * sdim), const),
            pl.BlockSpec((1, sdim), const),
            pl.BlockSpec((1, sdim), const),
        ],
        out_specs=[pl.BlockSpec((tm, sdim), row), pl.BlockSpec((tm, sdim), row)],
        out_shape=[jax.ShapeDtypeStruct((T, sdim), BF16), jax.ShapeDtypeStruct((T, sdim), BF16)],
        compiler_params=_params("parallel"),
        name="inproj_odd",
    )(x, g.reshape(1, D), w, ln_g.reshape(1, sdim), ln_b.reshape(1, sdim))


def _sgu_out_kernel(u_ref, v_ref, sw_ref, sb_ref, w_ref, x_ref, o_ref, g_ref, *, chunk, heads):
    tm, sdim = u_ref.shape
    hd = sdim // heads
    for n in range(tm // chunk):
        rows = slice(n * chunk, (n + 1) * chunk)
        for h in range(heads):
            lanes = slice(h * hd, (h + 1) * hd)
            sv = jnp.dot(sw_ref[h], v_ref[rows, lanes], preferred_element_type=F32) + sb_ref[h]
            g_ref[rows, lanes] = (u_ref[rows, lanes].astype(F32) * sv).astype(BF16)
    o_ref[...] = x_ref[...] + jnp.dot(g_ref[...], w_ref[...], preferred_element_type=F32)


def _sgu_out(u, v, sgu_w, sgu_b, w, x):
    T, sdim = u.shape
    D = x.shape[1]
    heads, chunk, _ = sgu_w.shape
    hd = sdim // heads
    tm = _pick_tile(T, 512)
    assert tm % chunk == 0
    sb = jnp.broadcast_to(sgu_b.astype(F32)[:, :, None], (heads, chunk, hd))
    row = lambda i: (i, 0)
    return pl.pallas_call(
        functools.partial(_sgu_out_kernel, chunk=chunk, heads=heads),
        grid=(T // tm,),
        in_specs=[
            pl.BlockSpec((tm, sdim), row),
            pl.BlockSpec((tm, sdim), row),
            pl.BlockSpec((heads, chunk, chunk), lambda i: (0, 0, 0)),
            pl.BlockSpec((heads, chunk, hd), lambda i: (0, 0, 0)),
            pl.BlockSpec((sdim, D), lambda i: (0, 0)),
            pl.BlockSpec((tm, D), row),
        ],
        out_specs=pl.BlockSpec((tm, D), row),
        out_shape=jax.ShapeDtypeStruct((T, D), F32),
        scratch_shapes=[pltpu.VMEM((tm, sdim), BF16)],
        compiler_params=_params("parallel"),
        name="sgu_outproj_odd",
    )(u, v, sgu_w, sb, w, x)


META_E1, META_E2, META_R1, META_R2, META_G1, META_G2 = range(6)
ROUTE_TILE = 512


def _router_kernel(x_ref, g_ref, wr_ref, meta_ref, cnt_ref, run_ref, *, n_experts):
    @pl.when(pl.program_id(0) == 0)
    def _():
        run_ref[...] = jnp.zeros_like(run_ref)

    xn = _rmsnorm_f32(x_ref[...], g_ref[...])
    xh = xn.astype(BF16)
    xl = (xn - xh.astype(F32)).astype(BF16)
    logits = (jnp.dot(xh, wr_ref[0], preferred_element_type=F32)
              + jnp.dot(xl, wr_ref[0], preferred_element_type=F32)
              + jnp.dot(xh, wr_ref[1], preferred_element_type=F32))
    tm = logits.shape[0]
    lane = lax.broadcasted_iota(jnp.int32, logits.shape, 1).astype(F32)
    neg = jnp.float32(-jnp.inf)
    big = jnp.float32(1e9)
    l1 = jnp.where(lane < n_experts, logits, neg)
    m1 = jnp.max(l1, axis=-1, keepdims=True)
    i1 = jnp.min(jnp.where(l1 == m1, lane, big), axis=-1, keepdims=True)
    l2 = jnp.where(lane == i1, neg, l1)
    m2 = jnp.max(l2, axis=-1, keepdims=True)
    i2 = jnp.min(jnp.where(l2 == m2, lane, big), axis=-1, keepdims=True)
    e2 = jnp.exp(m2 - m1)
    den = 1.0 + e2
    g1 = 1.0 / den
    g2 = e2 / den

    oh1 = jnp.where(lane == i1, 1.0, 0.0)
    oh2 = jnp.where(lane == i2, 1.0, 0.0)
    cnt = oh1 + oh2
    ri = lax.broadcasted_iota(jnp.int32, (tm, tm), 0)
    ci = lax.broadcasted_iota(jnp.int32, (tm, tm), 1)
    lower = jnp.where(ci < ri, 1.0, 0.0).astype(BF16)
    prefix = jnp.dot(lower, cnt.astype(BF16), preferred_element_type=F32) + run_ref[...]
    r1 = jnp.sum(oh1 * prefix, axis=-1, keepdims=True)
    r2 = jnp.sum(oh2 * prefix, axis=-1, keepdims=True)
    run_ref[...] += jnp.sum(cnt, axis=0, keepdims=True)

    meta = jnp.zeros_like(logits)
    for idx, val in ((META_E1, i1), (META_E2, i2), (META_R1, r1), (META_R2, r2), (META_G1, g1), (META_G2, g2)):
        meta = jnp.where(lane == idx, val, meta)
    meta_ref[...] = meta
    cnt_ref[...] = jnp.broadcast_to(run_ref[...], cnt_ref.shape)


def _router(x, g, w_router):
    T, D = x.shape
    E = w_router.shape[1]
    wr = jnp.zeros((D, LANES), F32).at[:, :E].set(w_router.astype(F32))
    wr_hi = wr.astype(BF16)
    wr = jnp.stack([wr_hi, (wr - wr_hi.astype(F32)).astype(BF16)])
    tm = ROUTE_TILE
    assert T % tm == 0
    row = lambda i: (i, 0)
    return pl.pallas_call(
        functools.partial(_router_kernel, n_experts=E),
        grid=(T // tm,),
        in_specs=[
            pl.BlockSpec((tm, D), row),
            pl.BlockSpec((1, D), lambda i: (0, 0)),
            pl.BlockSpec((2, D, LANES), lambda i: (0, 0, 0)),
        ],
        out_specs=[pl.BlockSpec((tm, LANES), row), pl.BlockSpec((SUBLANES, LANES), lambda i: (0, 0))],
        out_shape=[jax.ShapeDtypeStruct((T, LANES), F32), jax.ShapeDtypeStruct((SUBLANES, LANES), F32)],
        scratch_shapes=[pltpu.VMEM((1, LANES), F32)],
        compiler_params=_params("arbitrary"),
        name="moe_router",
    )(x, g.reshape(1, D), wr)


def _route_plan(meta, cnt, n_experts, tmx, n_tiles):
    counts = cnt[0, :n_experts].astype(jnp.int32)
    nt = (counts + tmx - 1) // tmx
    tend = jnp.cumsum(nt)
    off = (tend - nt) * tmx
    eids = jnp.arange(n_experts, dtype=jnp.int32)

    def pos(e_lane, r_lane):
        e = meta[:, e_lane].astype(jnp.int32)
        base = jnp.sum(jnp.where(e[:, None] == eids[None, :], off[None, :], 0), axis=1)
        return base + meta[:, r_lane].astype(jnp.int32)

    T = meta.shape[0]
    tm = ROUTE_TILE
    p1 = pos(META_E1, META_R1).reshape(T // tm, 1, tm)
    p2 = pos(META_E2, META_R2).reshape(T // tm, 1, tm)
    tiles = jnp.arange(n_tiles, dtype=jnp.int32)
    tile_expert = jnp.minimum(jnp.sum((tend[None, :] <= tiles[:, None]).astype(jnp.int32), axis=1), n_experts - 1)
    plan = dict(pos=jnp.concatenate([p1, p2], axis=2), tile_expert=tile_expert.astype(jnp.int32),
                n_active=tend[-1:].astype(jnp.int32), pad_start=(off + counts).astype(jnp.int32),
                pad_len=(nt * tmx - counts).astype(jnp.int32))
    return plan


def _dispatch_kernel(pstart_ref, plen_ref, na_ref, pos_ref, x_ref, g_ref, xs_ref, xn_ref, z_ref, sem, zsem,
                     *, pad_bits, n_steps, tmx):
    i = pl.program_id(0)
    tm = x_ref.shape[0]
    slot = i % 2

    @pl.when(i == 0)
    def _():
        z_ref[...] = jnp.zeros_like(z_ref)

        def zero_rows(dst, rows):
            cp = pltpu.make_async_copy(z_ref.at[pl.ds(0, rows)], xs_ref.at[pl.ds(dst, rows)], zsem)
            cp.start()
            cp.wait()

        for e in range(pstart_ref.shape[0]):
            n = plen_ref[e]
            s = pstart_ref[e]
            head = n & (SUBLANES - 1)
            for j in range(SUBLANES - 1):
                @pl.when(j < head)
                def _(j=j, s=s):
                    zero_rows(s + j, 1)

            groups = n >> 3
            base = s + head
            for k in reversed(range(pad_bits - 3)):
                @pl.when(((groups >> k) & 1) == 1)
                def _(k=k, groups=groups, base=base):
                    dst = pl.multiple_of(base + (((groups >> (k + 1)) << (k + 1)) << 3), SUBLANES)
                    zero_rows(dst, SUBLANES << k)

        zrows = z_ref.shape[0]
        n_tiles = xs_ref.shape[0] // tmx
        for t in range(n_tiles - pstart_ref.shape[0], n_tiles):
            @pl.when(t >= na_ref[0])
            def _(t=t):
                for c in range(tmx // zrows):
                    zero_rows(t * tmx + c * zrows, zrows)

    xn = _rmsnorm_f32(x_ref[...], g_ref[...])

    for s in range(2):
        @pl.when(slot == s)
        def _(s=s):
            xn_ref[s] = xn

            def issue(q, carry):
                r0 = pl.multiple_of(q * SUBLANES, SUBLANES)
                for j in range(SUBLANES):
                    src = xn_ref.at[s, pl.ds(r0 + j, 1)]
                    pltpu.make_async_copy(src, xs_ref.at[pl.ds(pos_ref[0, r0 + j], 1)], sem.at[s]).start()
                    pltpu.make_async_copy(src, xs_ref.at[pl.ds(pos_ref[0, tm + r0 + j], 1)], sem.at[s]).start()
                return carry

            lax.fori_loop(0, tm // SUBLANES, issue, 0)

    def drain(s):
        for _ in range(2):
            pltpu.make_async_copy(xn_ref.at[s], xs_ref.at[pl.ds(0, tm)], sem.at[s]).wait()

    @pl.when(i > 0)
    def _():
        drain(1 - slot)

    @pl.when(i == n_steps - 1)
    def _():
        drain(slot)


def _dispatch(x, g, plan, n_rows, tmx):
    T, D = x.shape
    tm = ROUTE_TILE
    pad_bits = (tmx - 1).bit_length()
    assert n_rows % tmx == 0 and tmx % (1 << (pad_bits - 1)) == 0
    grid_spec = pltpu.PrefetchScalarGridSpec(
        num_scalar_prefetch=3,
        grid=(T // tm,),
        in_specs=[
            pl.BlockSpec((None, 1, 2 * tm), lambda i, ps, pn, na: (i, 0, 0), memory_space=pltpu.SMEM),
            pl.BlockSpec((tm, D), lambda i, ps, pn, na: (i, 0)),
            pl.BlockSpec((1, D), lambda i, ps, pn, na: (0, 0)),
        ],
        out_specs=pl.BlockSpec(memory_space=pl.ANY),
        scratch_shapes=[
            pltpu.VMEM((2, tm, D), F32),
            pltpu.VMEM((1 << (pad_bits - 1), D), F32),
            pltpu.SemaphoreType.DMA((2,)),
            pltpu.SemaphoreType.DMA(()),
        ],
    )
    return pl.pallas_call(
        functools.partial(_dispatch_kernel, pad_bits=pad_bits, n_steps=T // tm, tmx=tmx),
        grid_spec=grid_spec,
        out_shape=jax.ShapeDtypeStruct((n_rows, D), F32),
        compiler_params=_params("arbitrary"),
        name="moe_dispatch",
    )(plan["pad_start"], plan["pad_len"], plan["n_active"], plan["pos"], x, g.reshape(1, D))


def _moe_experts_kernel(te_ref, na_ref, x_ref, w1_ref, w3_ref, w2_ref, o_ref, xb_ref):
    del te_ref
    f = pl.program_id(1)
    active = pl.program_id(0) < na_ref[0]

    @pl.when(f == 0)
    def _():
        o_ref[...] = jnp.zeros_like(o_ref)

    @pl.when((f == 0) & active)
    def _():
        xb_ref[...] = x_ref[...].astype(BF16)

    @pl.when(active)
    def _():
        xn = xb_ref[...]
        a = jnp.dot(xn, w1_ref[...], preferred_element_type=F32)
        b = jnp.dot(xn, w3_ref[...], preferred_element_type=F32)
        act = (a * _sigmoid(a) * b).astype(BF16)
        o_ref[...] += jnp.dot(act, w2_ref[...], preferred_element_type=F32)


def _moe_experts(xs, tile_expert, n_active, w1, w3, w2, *, tmx, tf):
    n_rows, D = xs.shape
    E, _, F = w1.shape
    assert F % tf == 0 and n_rows % tmx == 0
    nf = F // tf
    fsel = lambda i, f, na: jnp.where(i < na[0], f, nf - 1)
    grid_spec = pltpu.PrefetchScalarGridSpec(
        num_scalar_prefetch=2,
        grid=(n_rows // tmx, nf),
        in_specs=[
            pl.BlockSpec((tmx, D), lambda i, f, te, na: (jnp.minimum(i, na[0] - 1), 0)),
            pl.BlockSpec((None, D, tf), lambda i, f, te, na: (te[i], 0, fsel(i, f, na))),
            pl.BlockSpec((None, D, tf), lambda i, f, te, na: (te[i], 0, fsel(i, f, na))),
            pl.BlockSpec((None, tf, D), lambda i, f, te, na: (te[i], fsel(i, f, na), 0)),
        ],
        out_specs=pl.BlockSpec((tmx, D), lambda i, f, te, na: (i, 0)),
        scratch_shapes=[pltpu.VMEM((tmx, D), BF16)],
    )
    return pl.pallas_call(
        _moe_experts_kernel,
        grid_spec=grid_spec,
        out_shape=jax.ShapeDtypeStruct((n_rows, D), F32),
        compiler_params=_params("parallel", "arbitrary"),
        name="moe_experts",
    )(tile_expert, n_active, xs, w1, w3, w2)


def _combine_kernel(pos_ref, posn_ref, meta_ref, res_ref, lnf_ref, y_ref, o_ref, ya_ref, yb_ref, sem, *, n_steps):
    i = pl.program_id(0)
    tm = res_ref.shape[0]
    slot = i % 2

    def gather(p_ref, s):
        def issue(q, carry):
            r0 = pl.multiple_of(q * SUBLANES, SUBLANES)
            for j in range(SUBLANES):
                r = r0 + j
                pltpu.make_async_copy(y_ref.at[pl.ds(p_ref[0, r], 1)], ya_ref.at[s, pl.ds(r, 1)], sem.at[s]).start()
                pltpu.make_async_copy(y_ref.at[pl.ds(p_ref[0, tm + r], 1)], yb_ref.at[s, pl.ds(r, 1)],
                                      sem.at[s]).start()
            return carry

        lax.fori_loop(0, tm // SUBLANES, issue, 0)

    @pl.when(i == 0)
    def _():
        gather(pos_ref, 0)

    for s in range(2):
        @pl.when((i + 1 < n_steps) & (slot == 1 - s))
        def _(s=s):
            gather(posn_ref, s)

    pltpu.make_async_copy(y_ref.at[pl.ds(0, tm)], ya_ref.at[slot], sem.at[slot]).wait()
    pltpu.make_async_copy(y_ref.at[pl.ds(0, tm)], yb_ref.at[slot], sem.at[slot]).wait()

    meta = meta_ref[...]
    g1 = meta[:, META_G1:META_G1 + 1]
    g2 = meta[:, META_G2:META_G2 + 1]
    h = res_ref[...] + g1 * ya_ref[slot] + g2 * yb_ref[slot]
    o_ref[...] = _rmsnorm_f32(h, lnf_ref[...])


def _combine(y, pos, meta, res, ln_final, t_len, t_off):
    D = res.shape[1]
    tm = ROUTE_TILE
    assert t_len % tm == 0 and t_off % tm == 0
    off = t_off // tm
    last = off + t_len // tm - 1
    return pl.pallas_call(
        functools.partial(_combine_kernel, n_steps=t_len // tm),
        grid=(t_len // tm,),
        in_specs=[
            pl.BlockSpec((None, 1, 2 * tm), lambda i: (off + i, 0, 0), memory_space=pltpu.SMEM),
            pl.BlockSpec((None, 1, 2 * tm), lambda i: (jnp.minimum(off + i + 1, last), 0, 0),
                         memory_space=pltpu.SMEM),
            pl.BlockSpec((tm, LANES), lambda i: (off + i, 0)),
            pl.BlockSpec((tm, D), lambda i: (off + i, 0)),
            pl.BlockSpec((1, D), lambda i: (0, 0)),
            pl.BlockSpec(memory_space=pl.ANY),
        ],
        out_specs=pl.BlockSpec((tm, D), lambda i: (i, 0)),
        out_shape=jax.ShapeDtypeStruct((t_len, D), F32),
        scratch_shapes=[pltpu.VMEM((2, tm, D), F32), pltpu.VMEM((2, tm, D), F32), pltpu.SemaphoreType.DMA((2,))],
        compiler_params=_params("arbitrary"),
        name="moe_combine",
    )(pos, pos, meta, res, ln_final.reshape(1, D), y)


MOE_ROW_TILE = 1024
MOE_FF_TILE = 896


def _even_mixer(x, p):
    B = x.shape[0]
    cdim = p["dw_conv_w"].shape[-1]
    ldim = p["rg_lam"].shape[-1]
    assert cdim * 2 == ldim and p["w_in_e"].shape[-1] == 2 * cdim + 2 * ldim
    z = _inproj_e(x, p["ln_mix_e"], p["w_in_e"])
    c = _conv_branch(z, p["dw_conv_w"], p["dw_conv_b"], p["conv_ln_g"], p["conv_ln_b"], B)
    lru = functools.partial(_lru_pass, z, batch=B, col_x=2, col_y=1)
    hb = lru(None, p["rg_conv_w"][1], p["rg_conv_b"][1], p["rg_w_a"][1], p["rg_b_a"][1],
             p["rg_w_i"][1], p["rg_b_i"][1], p["rg_lam"][1], reverse=True)
    r = lru(hb, p["rg_conv_w"][0], p["rg_conv_b"][0], p["rg_w_a"][0], p["rg_b_a"][0],
            p["rg_w_i"][0], p["rg_b_i"][0], p["rg_lam"][0], reverse=False)
    return _outproj_e(c, r, x, p["w_out_e"])


def _trunks(xs, p):
    D = xs[0].shape[-1]
    lens = [x.shape[0] * x.shape[1] for x in xs]
    offs = [sum(lens[:k]) for k in range(len(xs))]
    T = sum(lens)

    h = _ffn([_even_mixer(x, p).reshape(n, D) for x, n in zip(xs, lens)],
             p["ln_ffn_e"], p["ffn_w1"], p["ffn_w3"], p["ffn_w2"])

    u, v = _inproj_o(h, p["ln_mix_o"], p["w_in_o"], p["sgu_ln_g"], p["sgu_ln_b"])
    h = _sgu_out(u, v, p["sgu_w"], p["sgu_b"], p["w_out_o"], h)

    E = p["w_router"].shape[1]
    tmx = MOE_ROW_TILE
    tf = MOE_FF_TILE if p["moe_w1"].shape[-1] % MOE_FF_TILE == 0 else _pick_tile(p["moe_w1"].shape[-1], 512)
    n_tiles = (2 * T) // tmx + E
    meta, cnt = _router(h, p["ln_ffn_o"], p["w_router"])
    plan = _route_plan(meta, cnt, E, tmx, n_tiles)
    xsort = _dispatch(h, p["ln_ffn_o"], plan, n_tiles * tmx, tmx)
    y = _moe_experts(xsort, plan["tile_expert"], plan["n_active"], p["moe_w1"], p["moe_w3"], p["moe_w2"],
                     tmx=tmx, tf=tf)
    return tuple(_combine(y, plan["pos"], meta, h, p["ln_final"], n, off).reshape(x.shape)
                 for x, n, off in zip(xs, lens, offs))


def kernel(x_prompt, x_sample, ln_mix_e, w_in_e, dw_conv_w, dw_conv_b, conv_ln_g, conv_ln_b, rg_conv_w, rg_conv_b, rg_w_a, rg_b_a, rg_w_i, rg_b_i, rg_lam, w_out_e, ln_ffn_e, ffn_w1, ffn_w3, ffn_w2, ln_mix_o, w_in_o, sgu_ln_g, sgu_ln_b, sgu_w, sgu_b, w_out_o, ln_ffn_o, w_router, moe_w1, moe_w3, moe_w2, ln_final):
    assert ln_mix_e.shape[0] == 1 and ln_mix_o.shape[0] == 1, "one even and one odd layer"
    bf = lambda a: a[0].astype(BF16)
    f32 = lambda a: a[0].astype(F32)
    p = {
        "ln_mix_e": f32(ln_mix_e), "w_in_e": bf(w_in_e),
        "dw_conv_w": f32(dw_conv_w), "dw_conv_b": f32(dw_conv_b),
        "conv_ln_g": f32(conv_ln_g), "conv_ln_b": f32(conv_ln_b),
        "rg_conv_w": f32(rg_conv_w), "rg_conv_b": f32(rg_conv_b),
        "rg_w_a": f32(rg_w_a), "rg_b_a": f32(rg_b_a), "rg_w_i": f32(rg_w_i), "rg_b_i": f32(rg_b_i),
        "rg_lam": f32(rg_lam), "w_out_e": bf(w_out_e),
        "ln_ffn_e": f32(ln_ffn_e), "ffn_w1": bf(ffn_w1), "ffn_w3": bf(ffn_w3), "ffn_w2": bf(ffn_w2),
        "ln_mix_o": f32(ln_mix_o), "w_in_o": bf(w_in_o),
        "sgu_ln_g": f32(sgu_ln_g), "sgu_ln_b": f32(sgu_ln_b),
        "sgu_w": bf(sgu_w), "sgu_b": f32(sgu_b), "w_out_o": bf(w_out_o),
        "ln_ffn_o": f32(ln_ffn_o), "w_router": f32(w_router),
        "moe_w1": bf(moe_w1), "moe_w3": bf(moe_w3), "moe_w2": bf(moe_w2),
        "ln_final": ln_final.astype(F32),
    }
    return _trunks([x_prompt, x_sample], p)
```

```python
import functools

import jax
import jax.numpy as jnp
from jax import lax
from jax.experimental import pallas as pl
from jax.experimental.pallas import tpu as pltpu

EPS = 1e-6
LRU_C = 8.0
LANES = 128
SUBLANES = 8
MXU_DIM = 256
VMEM_LIMIT_BYTES = 56 * 1024 * 1024
F32 = jnp.float32
BF16 = jnp.bfloat16


def _params(*sem):
    return pltpu.CompilerParams(dimension_semantics=sem, vmem_limit_bytes=VMEM_LIMIT_BYTES)


def _sigmoid(x):
    return 0.5 * (jnp.tanh(0.5 * x) + 1.0)


def _gelu_tanh(x):
    c = 0.7978845608028654
    return 0.5 * x * (1.0 + jnp.tanh(c * (x + 0.044715 * (x * x * x))))


def _rmsnorm_f32(x, g):
    return x * lax.rsqrt(jnp.mean(x * x, axis=-1, keepdims=True) + EPS) * g


def _pick_tile(n, want):
    t = min(n, want)
    while n % t:
        t //= 2
    return t


def _perm_matrix(batch):
    tl = MXU_DIM // batch
    r = jnp.arange(MXU_DIM)
    src = (r % batch) * tl + r // batch
    return (src[:, None] == r[None, :]).astype(BF16)


def _inproj_e_kernel(x_ref, g_ref, p_ref, w_ref, o_ref, xt_ref, *, n_chunk):
    B, tt, _ = x_ref.shape
    tl = MXU_DIM // B
    for q in range(tt // tl):
        xb = jnp.concatenate([x_ref[b, q * tl:(q + 1) * tl, :] for b in range(B)], axis=0)
        xn = _rmsnorm_f32(xb, g_ref[...]).astype(BF16)
        xt_ref[q * MXU_DIM:(q + 1) * MXU_DIM, :] = jnp.dot(
            p_ref[...], xn, preferred_element_type=F32).astype(BF16)
    xt = xt_ref[...]
    n = o_ref.shape[-1]
    for j in range(n // n_chunk):
        cols = slice(j * n_chunk, (j + 1) * n_chunk)
        o_ref[:, cols] = jnp.dot(xt, w_ref[:, cols], preferred_element_type=F32).astype(o_ref.dtype)


def _inproj_e(x, g, w):
    B, S, D = x.shape
    N = w.shape[1]
    assert MXU_DIM % B == 0
    tt = _pick_tile(S, 128)
    assert tt % (MXU_DIM // B) == 0
    return pl.pallas_call(
        functools.partial(_inproj_e_kernel, n_chunk=1024),
        grid=(S // tt,),
        in_specs=[
            pl.BlockSpec((B, tt, D), lambda i: (0, i, 0)),
            pl.BlockSpec((1, D), lambda i: (0, 0)),
            pl.BlockSpec((MXU_DIM, MXU_DIM), lambda i: (0, 0)),
            pl.BlockSpec((D, N), lambda i: (0, 0)),
        ],
        out_specs=pl.BlockSpec((tt * B, N), lambda i: (i, 0)),
        out_shape=jax.ShapeDtypeStruct((S * B, N), BF16),
        scratch_shapes=[pltpu.VMEM((tt * B, D), BF16)],
        compiler_params=_params("parallel"),
        name="inproj_even",
    )(x, g.reshape(1, D), _perm_matrix(B), w)


def _conv_kernel(zp_ref, zm_ref, zn_ref, w_ref, b_ref, g_ref, be_ref, o_ref, s_ref, y_ref,
                 *, rows, halo, cdim, ktaps, batch, unroll, ln_rows, n_steps):
    i = pl.program_id(0)
    n = n_steps

    def glu(z):
        return z[:, :cdim].astype(F32) * _sigmoid(z[:, cdim:].astype(F32))

    s_ref[halo:halo + rows, :] = glu(zm_ref[...])
    s_ref[0:halo, :] = jnp.where(i > 0, glu(zp_ref[...]), 0.0)
    s_ref[halo + rows:, :] = jnp.where(i < n - 1, glu(zn_ref[...]), 0.0)

    pad = ktaps // 2
    base = halo - pad * batch
    step = batch * unroll
    for j in range(cdim // LANES):
        lanes = slice(j * LANES, (j + 1) * LANES)
        wj = [jnp.broadcast_to(w_ref[k:k + 1, lanes], (batch, LANES)) for k in range(ktaps)]
        bj = jnp.broadcast_to(b_ref[:, lanes], (batch, LANES))

        def body(ci, carry, lanes=lanes, wj=wj, bj=bj):
            r0 = pl.multiple_of(ci * step, step)
            accs = [bj] * unroll
            for m in range(unroll + ktaps - 1):
                xm = s_ref[pl.ds(r0 + base + batch * m, batch), lanes]
                for u in range(unroll):
                    k = m - u
                    if 0 <= k < ktaps:
                        accs[u] = accs[u] + wj[k] * xm
            for u in range(unroll):
                y_ref[pl.ds(r0 + batch * u, batch), lanes] = accs[u]
            return carry

        lax.fori_loop(0, rows // step, body, 0)

    def ln_body(ci, carry):
        r0 = pl.multiple_of(ci * ln_rows, ln_rows)
        y = y_ref[pl.ds(r0, ln_rows), :]
        mu = jnp.mean(y, axis=-1, keepdims=True)
        d = y - mu
        var = jnp.mean(d * d, axis=-1, keepdims=True)
        yn = d * lax.rsqrt(var + EPS) * g_ref[...] + be_ref[...]
        o_ref[pl.ds(r0, ln_rows), :] = (yn * _sigmoid(yn)).astype(o_ref.dtype)
        return carry

    lax.fori_loop(0, rows // ln_rows, ln_body, 0)


def _conv_branch(z, w, b, ln_g, ln_b, batch):
    R = z.shape[0]
    ktaps, cdim = w.shape
    halo = 128
    assert (ktaps // 2) * batch <= halo and batch == SUBLANES
    rows = _pick_tile(R, 2048)
    assert rows % halo == 0
    hb = rows // halo
    nh = R // halo
    vec = lambda a: a.reshape(1, cdim)
    return pl.pallas_call(
        functools.partial(_conv_kernel, rows=rows, halo=halo, cdim=cdim, ktaps=ktaps, batch=batch,
                          unroll=8, ln_rows=min(rows, 256), n_steps=R // rows),
        grid=(R // rows,),
        in_specs=[
            pl.BlockSpec((halo, 2 * cdim), lambda i: (jnp.maximum(i * hb - 1, 0), 0)),
            pl.BlockSpec((rows, 2 * cdim), lambda i: (i, 0)),
            pl.BlockSpec((halo, 2 * cdim), lambda i: (jnp.minimum((i + 1) * hb, nh - 1), 0)),
            pl.BlockSpec((ktaps, cdim), lambda i: (0, 0)),
            pl.BlockSpec((1, cdim), lambda i: (0, 0)),
            pl.BlockSpec((1, cdim), lambda i: (0, 0)),
            pl.BlockSpec((1, cdim), lambda i: (0, 0)),
        ],
        out_specs=pl.BlockSpec((rows, cdim), lambda i: (i, 0)),
        out_shape=jax.ShapeDtypeStruct((R, cdim), BF16),
        scratch_shapes=[pltpu.VMEM((rows + 2 * halo, cdim), F32), pltpu.VMEM((rows, cdim), F32)],
        compiler_params=_params("parallel"),
        name="conv_branch",
    )(z, z, z, w, vec(b), vec(ln_g), vec(ln_b))


def _lru_kernel(*refs, rows, batch, ktaps, sub, reverse, final):
    if final:
        (zx_ref, zy_ref, hb_ref, cw_ref, cb_ref, wg_ref, ba_ref, bi_ref, lam_ref,
         o_ref, xs_ref, a_ref, u_ref, h_ref) = refs
    else:
        (zx_ref, cw_ref, cb_ref, wg_ref, ba_ref, bi_ref, lam_ref,
         o_ref, xs_ref, a_ref, u_ref, h_ref) = refs
    hl = (ktaps - 1) * batch
    main = 0 if reverse else hl
    ldim = zx_ref.shape[-1]
    ngrp = wg_ref.shape[0]
    gw = ldim // ngrp

    @pl.when(pl.program_id(0) == 0)
    def _():
        h_ref[...] = jnp.zeros_like(h_ref)
        xs_ref[...] = jnp.zeros_like(xs_ref)

    xs_ref[main:main + rows, :] = zx_ref[...].astype(F32)

    lam = lam_ref[...]
    sp = jnp.maximum(-lam, 0.0) + jnp.log(1.0 + jnp.exp(-jnp.abs(lam)))
    half_neg_c_sp = -0.5 * LRU_C * sp

    def gate_body(ci, carry):
        r0 = pl.multiple_of(ci * sub, sub)
        xc = jnp.broadcast_to(cb_ref[...], (sub, ldim))
        for k in range(ktaps):
            xc = xc + cw_ref[k:k + 1, :] * xs_ref[pl.ds(r0 + batch * k, sub), :]
        xcb = xc.astype(BF16)
        for g in range(ngrp):
            cols = slice(g * gw, (g + 1) * gw)
            gates = jnp.dot(xcb[:, cols], wg_ref[g], preferred_element_type=F32)
            tr = jnp.tanh(gates[:, :gw] + ba_ref[:, cols])
            ig = 0.5 * (jnp.tanh(gates[:, gw:] + bi_ref[:, cols]) + 1.0)
            log_a = (tr + 1.0) * half_neg_c_sp[:, cols]
            a_ref[pl.ds(r0, sub), cols] = jnp.exp(log_a)
            th = jnp.tanh(log_a)
            mult = jnp.sqrt(jnp.maximum(-2.0 * th / (1.0 - th), 1e-12))
            u_ref[pl.ds(r0, sub), cols] = mult * (ig * xc[:, cols])
        return carry

    lax.fori_loop(0, rows // sub, gate_body, 0)

    if reverse:
        xs_ref[rows:rows + hl, :] = xs_ref[0:hl, :]
    else:
        xs_ref[0:hl, :] = xs_ref[rows:rows + hl, :]

    nt = rows // batch

    def scan_body(t, h):
        tt = (nt - 1 - t) if reverse else t
        r = pl.multiple_of(tt * batch, batch)
        h = a_ref[pl.ds(r, batch), :] * h + u_ref[pl.ds(r, batch), :]
        u_ref[pl.ds(r, batch), :] = h
        return h

    h_ref[...] = lax.fori_loop(0, nt, scan_body, h_ref[...], unroll=8)

    def out_body(ci, carry):
        r0 = pl.multiple_of(ci * sub, sub)
        h = u_ref[pl.ds(r0, sub), :]
        if final:
            h = h + hb_ref[pl.ds(r0, sub), :].astype(F32)
            h = _gelu_tanh(zy_ref[pl.ds(r0, sub), :].astype(F32)) * h
        o_ref[pl.ds(r0, sub), :] = h.astype(o_ref.dtype)
        return carry

    lax.fori_loop(0, rows // sub, out_body, 0)


def _pack_blockdiag(w, width):
    H, dh, _ = w.shape
    per = width // dh
    G = H // per
    w = w.reshape(G, per, dh, dh)
    eye = jnp.eye(per, dtype=w.dtype)
    return jnp.einsum("pq,gpij->gpiqj", eye, w).reshape(G, per * dh, per * dh)


def _lru_pass(z, hb, cw, cb, w_a, b_a, w_i, b_i, lam, *, batch, reverse, col_x, col_y):
    R = z.shape[0]
    ktaps, ldim = cw.shape
    final = hb is not None
    rows = _pick_tile(R, 1024)
    nblk = R // rows
    wg = (0.5 * jnp.concatenate([_pack_blockdiag(w_a, MXU_DIM), _pack_blockdiag(w_i, MXU_DIM)], axis=-1)).astype(BF16)
    b_a, b_i = 0.5 * b_a, 0.5 * b_i
    ngrp = wg.shape[0]
    vec = lambda a: a.reshape(1, ldim).astype(F32)
    blk = (lambda i: nblk - 1 - i) if reverse else (lambda i: i)
    const2 = lambda i: (0, 0)
    in_specs = [pl.BlockSpec((rows, ldim), lambda i: (blk(i), col_x))]
    args = [z]
    if final:
        in_specs += [pl.BlockSpec((rows, ldim), lambda i: (blk(i), col_y)),
                     pl.BlockSpec((rows, ldim), lambda i: (blk(i), 0))]
        args += [z, hb]
    in_specs += [
        pl.BlockSpec((ktaps, ldim), const2),
        pl.BlockSpec((1, ldim), const2),
        pl.BlockSpec((ngrp, MXU_DIM, 2 * MXU_DIM), lambda i: (0, 0, 0)),
        pl.BlockSpec((1, ldim), const2),
        pl.BlockSpec((1, ldim), const2),
        pl.BlockSpec((1, ldim), const2),
    ]
    args += [cw, vec(cb), wg, vec(b_a), vec(b_i), vec(lam)]
    hl = (ktaps - 1) * batch
    return pl.pallas_call(
        functools.partial(_lru_kernel, rows=rows, batch=batch, ktaps=ktaps, sub=min(rows, 256),
                          reverse=reverse, final=final),
        grid=(nblk,),
        in_specs=in_specs,
        out_specs=pl.BlockSpec((rows, ldim), lambda i: (blk(i), 0)),
        out_shape=jax.ShapeDtypeStruct((R, ldim), BF16),
        scratch_shapes=[
            pltpu.VMEM((rows + hl, ldim), F32),
            pltpu.VMEM((rows, ldim), F32),
            pltpu.VMEM((rows, ldim), F32),
            pltpu.VMEM((batch, ldim), F32),
        ],
        compiler_params=_params("arbitrary"),
        name="rglru_bwd" if reverse else "rglru_fwd",
    )(*args)


def _outproj_e_kernel(c_ref, r_ref, x_ref, pt_ref, w_ref, o_ref, m_ref):
    B, tt, _ = x_ref.shape
    tl = MXU_DIM // B
    cdim = c_ref.shape[1]
    for q in range(tt // tl):
        rows = slice(q * MXU_DIM, (q + 1) * MXU_DIM)
        m_ref[rows, :cdim] = jnp.dot(pt_ref[...], c_ref[rows, :], preferred_element_type=F32).astype(BF16)
        m_ref[rows, cdim:] = jnp.dot(pt_ref[...], r_ref[rows, :], preferred_element_type=F32).astype(BF16)
    acc = jnp.dot(m_ref[...], w_ref[...], preferred_element_type=F32)
    for q in range(tt // tl):
        for b in range(B):
            r0 = q * MXU_DIM + b * tl
            ts = slice(q * tl, (q + 1) * tl)
            o_ref[b, ts, :] = x_ref[b, ts, :] + acc[r0:r0 + tl, :]


def _outproj_e(c, r, x, w):
    B, S, D = x.shape
    cdim, ldim = c.shape[1], r.shape[1]
    tt = _pick_tile(S, 128)
    return pl.pallas_call(
        _outproj_e_kernel,
        grid=(S // tt,),
        in_specs=[
            pl.BlockSpec((tt * B, cdim), lambda i: (i, 0)),
            pl.BlockSpec((tt * B, ldim), lambda i: (i, 0)),
            pl.BlockSpec((B, tt, D), lambda i: (0, i, 0)),
            pl.BlockSpec((MXU_DIM, MXU_DIM), lambda i: (0, 0)),
            pl.BlockSpec((cdim + ldim, D), lambda i: (0, 0)),
        ],
        out_specs=pl.BlockSpec((B, tt, D), lambda i: (0, i, 0)),
        out_shape=jax.ShapeDtypeStruct((B, S, D), F32),
        scratch_shapes=[pltpu.VMEM((tt * B, cdim + ldim), BF16)],
        compiler_params=_params("parallel"),
        name="outproj_even",
    )(c, r, x, _perm_matrix(B).T, w)


FFN_COL_CHUNK = 6 * MXU_DIM


def _ffn_kernel(*refs, bounds):
    n_src = len(bounds) - 1
    x_refs = refs[:n_src]
    g_ref, w1_ref, w3_ref, w2_ref, o_ref, xn_ref = refs[n_src:]
    i = pl.program_id(0)
    for x_ref, lo, hi in zip(x_refs, bounds[:-1], bounds[1:]):
        @pl.when((i >= lo) & (i < hi))
        def _(x_ref=x_ref):
            x = x_ref[...]
            xn_ref[...] = _rmsnorm_f32(x, g_ref[...]).astype(BF16)
            o_ref[...] = x

    xn = xn_ref[...]
    F = w1_ref.shape[1]
    acc = None
    for c0 in range(0, F, FFN_COL_CHUNK):
        c1 = min(c0 + FFN_COL_CHUNK, F)
        a = jnp.dot(xn, w1_ref[:, c0:c1], preferred_element_type=F32)
        b = jnp.dot(xn, w3_ref[:, c0:c1], preferred_element_type=F32)
        act = (a * _sigmoid(a) * b).astype(BF16)
        part = jnp.dot(act, w2_ref[c0:c1, :], preferred_element_type=F32)
        acc = part if acc is None else acc + part
    o_ref[...] += acc


def _ffn(xs, g, w1, w3, w2, *, tm=512):
    D = xs[0].shape[1]
    F = w1.shape[1]
    assert all(x.shape[0] % tm == 0 for x in xs)
    counts = [x.shape[0] // tm for x in xs]
    starts = [sum(counts[:k]) for k in range(len(xs))]
    T = tm * sum(counts)

    def src_spec(lo, n):
        return pl.BlockSpec((tm, D), lambda i: (jnp.clip(i - lo, 0, n - 1), 0))

    resident = dict(pipeline_mode=pl.Buffered(1))
    return pl.pallas_call(
        functools.partial(_ffn_kernel, bounds=tuple(starts) + (T // tm,)),
        grid=(T // tm,),
        in_specs=[src_spec(lo, n) for lo, n in zip(starts, counts)] + [
            pl.BlockSpec((1, D), lambda i: (0, 0)),
            pl.BlockSpec((D, F), lambda i: (0, 0), **resident),
            pl.BlockSpec((D, F), lambda i: (0, 0), **resident),
            pl.BlockSpec((F, D), lambda i: (0, 0), **resident),
        ],
        out_specs=pl.BlockSpec((tm, D), lambda i: (i, 0)),
        out_shape=jax.ShapeDtypeStruct((T, D), F32),
        scratch_shapes=[pltpu.VMEM((tm, D), BF16)],
        compiler_params=_params("parallel"),
        name="ffn_swiglu",
    )(*xs, g.reshape(1, D), w1, w3, w2)


def _inproj_o_kernel(x_ref, g_ref, w_ref, lg_ref, lb_ref, u_ref, v_ref, zv_ref, *, sdim, n_chunk):
    xn = _rmsnorm_f32(x_ref[...], g_ref[...]).astype(BF16)
    for j in range(sdim // n_chunk):
        cols = slice(j * n_chunk, (j + 1) * n_chunk)
        zu = _gelu_tanh(jnp.dot(xn, w_ref[:, cols], preferred_element_type=F32))
        u_ref[:, cols] = zu.astype(u_ref.dtype)
    for j in range(sdim // n_chunk):
        cols = slice(j * n_chunk, (j + 1) * n_chunk)
        wcols = slice(sdim + j * n_chunk, sdim + (j + 1) * n_chunk)
        zv_ref[:, cols] = _gelu_tanh(jnp.dot(xn, w_ref[:, wcols], preferred_element_type=F32))
    zv = zv_ref[...]
    mu = jnp.mean(zv, axis=-1, keepdims=True)
    d = zv - mu
    var = jnp.mean(d * d, axis=-1, keepdims=True)
    v_ref[...] = (d * lax.rsqrt(var + EPS) * lg_ref[...] + lb_ref[...]).astype(v_ref.dtype)


def _inproj_o(x, g, w, ln_g, ln_b):
    T, D = x.shape
    sdim = w.shape[1] // 2
    tm = _pick_tile(T, 512)
    row = lambda i: (i, 0)
    const = lambda i: (0, 0)
    return pl.pallas_call(
        functools.partial(_inproj_o_kernel, sdim=sdim, n_chunk=_pick_tile(sdim, 512)),
        grid=(T // tm,),
        scratch_shapes=[pltpu.VMEM((tm, sdim), F32)],
        in_specs=[
            pl.BlockSpec((tm, D), row),
            pl.BlockSpec((1, D), const),
            pl.BlockSpec((D, 2 * sdim), const),
            pl.BlockSpec((1, sdim), const),
            pl.BlockSpec((1, sdim), const),
        ],
        out_specs=[pl.BlockSpec((tm, sdim), row), pl.BlockSpec((tm, sdim), row)],
        out_shape=[jax.ShapeDtypeStruct((T, sdim), BF16), jax.ShapeDtypeStruct((T, sdim), BF16)],
        compiler_params=_params("parallel"),
        name="inproj_odd",
    )(x, g.reshape(1, D), w, ln_g.reshape(1, sdim), ln_b.reshape(1, sdim))


def _sgu_out_kernel(u_ref, v_ref, sw_ref, sb_ref, w_ref, x_ref, o_ref, g_ref, *, chunk, heads):
    tm, sdim = u_ref.shape
    hd = sdim // heads
    for n in range(tm // chunk):
        rows = slice(n * chunk, (n + 1) * chunk)
        for h in range(heads):
            lanes = slice(h * hd, (h + 1) * hd)
            sv = jnp.dot(sw_ref[h], v_ref[rows, lanes], preferred_element_type=F32) + sb_ref[h]
            g_ref[rows, lanes] = (u_ref[rows, lanes].astype(F32) * sv).astype(BF16)
    o_ref[...] = x_ref[...] + jnp.dot(g_ref[...], w_ref[...], preferred_element_type=F32)


def _sgu_out(u, v, sgu_w, sgu_b, w, x):
    T, sdim = u.shape
    D = x.shape[1]
    heads, chunk, _ = sgu_w.shape
    hd = sdim // heads
    tm = _pick_tile(T, 512)
    assert tm % chunk == 0
    sb = jnp.broadcast_to(sgu_b.astype(F32)[:, :, None], (heads, chunk, hd))
    row = lambda i: (i, 0)
    return pl.pallas_call(
        functools.partial(_sgu_out_kernel, chunk=chunk, heads=heads),
        grid=(T // tm,),
        in_specs=[
            pl.BlockSpec((tm, sdim), row),
            pl.BlockSpec((tm, sdim), row),
            pl.BlockSpec((heads, chunk, chunk), lambda i: (0, 0, 0)),
            pl.BlockSpec((heads, chunk, hd), lambda i: (0, 0, 0)),
            pl.BlockSpec((sdim, D), lambda i: (0, 0)),
            pl.BlockSpec((tm, D), row),
        ],
        out_specs=pl.BlockSpec((tm, D), row),
        out_shape=jax.ShapeDtypeStruct((T, D), F32),
        scratch_shapes=[pltpu.VMEM((tm, sdim), BF16)],
        compiler_params=_params("parallel"),
        name="sgu_outproj_odd",
    )(u, v, sgu_w, sb, w, x)


META_E1, META_E2, META_R1, META_R2, META_G1, META_G2 = range(6)
ROUTE_TILE = 512


def _router_kernel(x_ref, g_ref, wr_ref, meta_ref, cnt_ref, run_ref, *, n_experts):
    @pl.when(pl.program_id(0) == 0)
    def _():
        run_ref[...] = jnp.zeros_like(run_ref)

    xn = _rmsnorm_f32(x_ref[...], g_ref[...])
    xh = xn.astype(BF16)
    xl = (xn - xh.astype(F32)).astype(BF16)
    logits = (jnp.dot(xh, wr_ref[0], preferred_element_type=F32)
              + jnp.dot(xl, wr_ref[0], preferred_element_type=F32)
              + jnp.dot(xh, wr_ref[1], preferred_element_type=F32))
    tm = logits.shape[0]
    lane = lax.broadcasted_iota(jnp.int32, logits.shape, 1).astype(F32)
    neg = jnp.float32(-jnp.inf)
    big = jnp.float32(1e9)
    l1 = jnp.where(lane < n_experts, logits, neg)
    m1 = jnp.max(l1, axis=-1, keepdims=True)
    i1 = jnp.min(jnp.where(l1 == m1, lane, big), axis=-1, keepdims=True)
    l2 = jnp.where(lane == i1, neg, l1)
    m2 = jnp.max(l2, axis=-1, keepdims=True)
    i2 = jnp.min(jnp.where(l2 == m2, lane, big), axis=-1, keepdims=True)
    e2 = jnp.exp(m2 - m1)
    den = 1.0 + e2
    g1 = 1.0 / den
    g2 = e2 / den

    oh1 = jnp.where(lane == i1, 1.0, 0.0)
    oh2 = jnp.where(lane == i2, 1.0, 0.0)
    cnt = oh1 + oh2
    ri = lax.broadcasted_iota(jnp.int32, (tm, tm), 0)
    ci = lax.broadcasted_iota(jnp.int32, (tm, tm), 1)
    lower = jnp.where(ci < ri, 1.0, 0.0).astype(BF16)
    prefix = jnp.dot(lower, cnt.astype(BF16), preferred_element_type=F32) + run_ref[...]
    r1 = jnp.sum(oh1 * prefix, axis=-1, keepdims=True)
    r2 = jnp.sum(oh2 * prefix, axis=-1, keepdims=True)
    run_ref[...] += jnp.sum(cnt, axis=0, keepdims=True)

    meta = jnp.zeros_like(logits)
    for idx, val in ((META_E1, i1), (META_E2, i2), (META_R1, r1), (META_R2, r2), (META_G1, g1), (META_G2, g2)):
        meta = jnp.where(lane == idx, val, meta)
    meta_ref[...] = meta
    cnt_ref[...] = jnp.broadcast_to(run_ref[...], cnt_ref.shape)


def _router(x, g, w_router):
    T, D = x.shape
    E = w_router.shape[1]
    wr = jnp.zeros((D, LANES), F32).at[:, :E].set(w_router.astype(F32))
    wr_hi = wr.astype(BF16)
    wr = jnp.stack([wr_hi, (wr - wr_hi.astype(F32)).astype(BF16)])
    tm = ROUTE_TILE
    assert T % tm == 0
    row = lambda i: (i, 0)
    return pl.pallas_call(
        functools.partial(_router_kernel, n_experts=E),
        grid=(T // tm,),
        in_specs=[
            pl.BlockSpec((tm, D), row),
            pl.BlockSpec((1, D), lambda i: (0, 0)),
            pl.BlockSpec((2, D, LANES), lambda i: (0, 0, 0)),
        ],
        out_specs=[pl.BlockSpec((tm, LANES), row), pl.BlockSpec((SUBLANES, LANES), lambda i: (0, 0))],
        out_shape=[jax.ShapeDtypeStruct((T, LANES), F32), jax.ShapeDtypeStruct((SUBLANES, LANES), F32)],
        scratch_shapes=[pltpu.VMEM((1, LANES), F32)],
        compiler_params=_params("arbitrary"),
        name="moe_router",
    )(x, g.reshape(1, D), wr)


def _route_plan(meta, cnt, n_experts, tmx, n_tiles):
    counts = cnt[0, :n_experts].astype(jnp.int32)
    nt = (counts + tmx - 1) // tmx
    tend = jnp.cumsum(nt)
    off = (tend - nt) * tmx
    eids = jnp.arange(n_experts, dtype=jnp.int32)

    def pos(e_lane, r_lane):
        e = meta[:, e_lane].astype(jnp.int32)
        base = jnp.sum(jnp.where(e[:, None] == eids[None, :], off[None, :], 0), axis=1)
        return base + meta[:, r_lane].astype(jnp.int32)

    T = meta.shape[0]
    tm = ROUTE_TILE
    p1 = pos(META_E1, META_R1).reshape(T // tm, 1, tm)
    p2 = pos(META_E2, META_R2).reshape(T // tm, 1, tm)
    tiles = jnp.arange(n_tiles, dtype=jnp.int32)
    tile_expert = jnp.minimum(jnp.sum((tend[None, :] <= tiles[:, None]).astype(jnp.int32), axis=1), n_experts - 1)
    plan = dict(pos=jnp.concatenate([p1, p2], axis=2), tile_expert=tile_expert.astype(jnp.int32),
                n_active=tend[-1:].astype(jnp.int32), pad_start=(off + counts).astype(jnp.int32),
                pad_len=(nt * tmx - counts).astype(jnp.int32))
    return plan


def _dispatch_kernel(pstart_ref, plen_ref, na_ref, pos_ref, x_ref, g_ref, xs_ref, xn_ref, z_ref, sem, zsem,
                     *, pad_bits, n_steps, tmx):
    i = pl.program_id(0)
    tm = x_ref.shape[0]
    slot = i % 2

    @pl.when(i == 0)
    def _():
        z_ref[...] = jnp.zeros_like(z_ref)

        def zero_rows(dst, rows):
            cp = pltpu.make_async_copy(z_ref.at[pl.ds(0, rows)], xs_ref.at[pl.ds(dst, rows)], zsem)
            cp.start()
            cp.wait()

        for e in range(pstart_ref.shape[0]):
            n = plen_ref[e]
            s = pstart_ref[e]
            head = n & (SUBLANES - 1)
            for j in range(SUBLANES - 1):
                @pl.when(j < head)
                def _(j=j, s=s):
                    zero_rows(s + j, 1)

            groups = n >> 3
            base = s + head
            for k in reversed(range(pad_bits - 3)):
                @pl.when(((groups >> k) & 1) == 1)
                def _(k=k, groups=groups, base=base):
                    dst = pl.multiple_of(base + (((groups >> (k + 1)) << (k + 1)) << 3), SUBLANES)
                    zero_rows(dst, SUBLANES << k)

        zrows = z_ref.shape[0]
        n_tiles = xs_ref.shape[0] // tmx
        for t in range(n_tiles - pstart_ref.shape[0], n_tiles):
            @pl.when(t >= na_ref[0])
            def _(t=t):
                for c in range(tmx // zrows):
                    zero_rows(t * tmx + c * zrows, zrows)

    xn = _rmsnorm_f32(x_ref[...], g_ref[...])

    for s in range(2):
        @pl.when(slot == s)
        def _(s=s):
            xn_ref[s] = xn

            def issue(q, carry):
                r0 = pl.multiple_of(q * SUBLANES, SUBLANES)
                for j in range(SUBLANES):
                    src = xn_ref.at[s, pl.ds(r0 + j, 1)]
                    pltpu.make_async_copy(src, xs_ref.at[pl.ds(pos_ref[0, r0 + j], 1)], sem.at[s]).start()
                    pltpu.make_async_copy(src, xs_ref.at[pl.ds(pos_ref[0, tm + r0 + j], 1)], sem.at[s]).start()
                return carry

            lax.fori_loop(0, tm // SUBLANES, issue, 0)

    def drain(s):
        for _ in range(2):
            pltpu.make_async_copy(xn_ref.at[s], xs_ref.at[pl.ds(0, tm)], sem.at[s]).wait()

    @pl.when(i > 0)
    def _():
        drain(1 - slot)

    @pl.when(i == n_steps - 1)
    def _():
        drain(slot)


def _dispatch(x, g, plan, n_rows, tmx):
    T, D = x.shape
    tm = ROUTE_TILE
    pad_bits = (tmx - 1).bit_length()
    assert n_rows % tmx == 0 and tmx % (1 << (pad_bits - 1)) == 0
    grid_spec = pltpu.PrefetchScalarGridSpec(
        num_scalar_prefetch=3,
        grid=(T // tm,),
        in_specs=[
            pl.BlockSpec((None, 1, 2 * tm), lambda i, ps, pn, na: (i, 0, 0), memory_space=pltpu.SMEM),
            pl.BlockSpec((tm, D), lambda i, ps, pn, na: (i, 0)),
            pl.BlockSpec((1, D), lambda i, ps, pn, na: (0, 0)),
        ],
        out_specs=pl.BlockSpec(memory_space=pl.ANY),
        scratch_shapes=[
            pltpu.VMEM((2, tm, D), F32),
            pltpu.VMEM((1 << (pad_bits - 1), D), F32),
            pltpu.SemaphoreType.DMA((2,)),
            pltpu.SemaphoreType.DMA(()),
        ],
    )
    return pl.pallas_call(
        functools.partial(_dispatch_kernel, pad_bits=pad_bits, n_steps=T // tm, tmx=tmx),
        grid_spec=grid_spec,
        out_shape=jax.ShapeDtypeStruct((n_rows, D), F32),
        compiler_params=_params("arbitrary"),
        name="moe_dispatch",
    )(plan["pad_start"], plan["pad_len"], plan["n_active"], plan["pos"], x, g.reshape(1, D))


MOE_COL_CHUNK = 4 * MXU_DIM


def _moe_experts_kernel(te_ref, na_ref, x_ref, w1_ref, w3_ref, w2_ref, o_ref, xb_ref):
    del te_ref
    f = pl.program_id(1)
    active = pl.program_id(0) < na_ref[0]

    @pl.when(f == 0)
    def _():
        o_ref[...] = jnp.zeros_like(o_ref)

    @pl.when((f == 0) & active)
    def _():
        xb_ref[...] = x_ref[...].astype(BF16)

    @pl.when(active)
    def _():
        xn = xb_ref[...]
        tf = w1_ref.shape[1]
        acc = None
        for c0 in range(0, tf, MOE_COL_CHUNK):
            c1 = min(c0 + MOE_COL_CHUNK, tf)
            a = jnp.dot(xn, w1_ref[:, c0:c1], preferred_element_type=F32)
            b = jnp.dot(xn, w3_ref[:, c0:c1], preferred_element_type=F32)
            act = (a * _sigmoid(a) * b).astype(BF16)
            part = jnp.dot(act, w2_ref[c0:c1, :], preferred_element_type=F32)
            acc = part if acc is None else acc + part
        o_ref[...] += acc


def _moe_experts(xs, tile_expert, n_active, w1, w3, w2, *, tmx, tf):
    n_rows, D = xs.shape
    E, _, F = w1.shape
    assert F % tf == 0 and n_rows % tmx == 0
    nf = F // tf
    fsel = lambda i, f, na: jnp.where(i < na[0], f, nf - 1)
    grid_spec = pltpu.PrefetchScalarGridSpec(
        num_scalar_prefetch=2,
        grid=(n_rows // tmx, nf),
        in_specs=[
            pl.BlockSpec((tmx, D), lambda i, f, te, na: (jnp.minimum(i, na[0] - 1), 0)),
            pl.BlockSpec((None, D, tf), lambda i, f, te, na: (te[i], 0, fsel(i, f, na))),
            pl.BlockSpec((None, D, tf), lambda i, f, te, na: (te[i], 0, fsel(i, f, na))),
            pl.BlockSpec((None, tf, D), lambda i, f, te, na: (te[i], fsel(i, f, na), 0)),
        ],
        out_specs=pl.BlockSpec((tmx, D), lambda i, f, te, na: (i, 0)),
        scratch_shapes=[pltpu.VMEM((tmx, D), BF16)],
    )
    return pl.pallas_call(
        _moe_experts_kernel,
        grid_spec=grid_spec,
        out_shape=jax.ShapeDtypeStruct((n_rows, D), F32),
        compiler_params=_params("parallel", "arbitrary"),
        name="moe_experts",
    )(tile_expert, n_active, xs, w1, w3, w2)


def _combine_kernel(pos_ref, posn_ref, meta_ref, res_ref, lnf_ref, y_ref, o_ref, ya_ref, yb_ref, sem, *, n_steps):
    i = pl.program_id(0)
    tm = res_ref.shape[0]
    slot = i % 2

    def gather(p_ref, s):
        def issue(q, carry):
            r0 = pl.multiple_of(q * SUBLANES, SUBLANES)
            for j in range(SUBLANES):
                r = r0 + j
                pltpu.make_async_copy(y_ref.at[pl.ds(p_ref[0, r], 1)], ya_ref.at[s, pl.ds(r, 1)], sem.at[s]).start()
                pltpu.make_async_copy(y_ref.at[pl.ds(p_ref[0, tm + r], 1)], yb_ref.at[s, pl.ds(r, 1)],
                                      sem.at[s]).start()
            return carry

        lax.fori_loop(0, tm // SUBLANES, issue, 0)

    @pl.when(i == 0)
    def _():
        gather(pos_ref, 0)

    for s in range(2):
        @pl.when((i + 1 < n_steps) & (slot == 1 - s))
        def _(s=s):
            gather(posn_ref, s)

    pltpu.make_async_copy(y_ref.at[pl.ds(0, tm)], ya_ref.at[slot], sem.at[slot]).wait()
    pltpu.make_async_copy(y_ref.at[pl.ds(0, tm)], yb_ref.at[slot], sem.at[slot]).wait()

    meta = meta_ref[...]
    g1 = meta[:, META_G1:META_G1 + 1]
    g2 = meta[:, META_G2:META_G2 + 1]
    h = res_ref[...] + g1 * ya_ref[slot] + g2 * yb_ref[slot]
    o_ref[...] = _rmsnorm_f32(h, lnf_ref[...])


def _combine(y, pos, meta, res, ln_final, t_len, t_off):
    D = res.shape[1]
    tm = ROUTE_TILE
    assert t_len % tm == 0 and t_off % tm == 0
    off = t_off // tm
    last = off + t_len // tm - 1
    return pl.pallas_call(
        functools.partial(_combine_kernel, n_steps=t_len // tm),
        grid=(t_len // tm,),
        in_specs=[
            pl.BlockSpec((None, 1, 2 * tm), lambda i: (off + i, 0, 0), memory_space=pltpu.SMEM),
            pl.BlockSpec((None, 1, 2 * tm), lambda i: (jnp.minimum(off + i + 1, last), 0, 0),
                         memory_space=pltpu.SMEM),
            pl.BlockSpec((tm, LANES), lambda i: (off + i, 0)),
            pl.BlockSpec((tm, D), lambda i: (off + i, 0)),
            pl.BlockSpec((1, D), lambda i: (0, 0)),
            pl.BlockSpec(memory_space=pl.ANY),
        ],
        out_specs=pl.BlockSpec((tm, D), lambda i: (i, 0)),
        out_shape=jax.ShapeDtypeStruct((t_len, D), F32),
        scratch_shapes=[pltpu.VMEM((2, tm, D), F32), pltpu.VMEM((2, tm, D), F32), pltpu.SemaphoreType.DMA((2,))],
        compiler_params=_params("arbitrary"),
        name="moe_combine",
    )(pos, pos, meta, res, ln_final.reshape(1, D), y)


MOE_ROW_TILE = 1024
MOE_FF_TILE = 1792


def _even_mixer(x, p):
    B = x.shape[0]
    cdim = p["dw_conv_w"].shape[-1]
    ldim = p["rg_lam"].shape[-1]
    assert cdim * 2 == ldim and p["w_in_e"].shape[-1] == 2 * cdim + 2 * ldim
    z = _inproj_e(x, p["ln_mix_e"], p["w_in_e"])
    c = _conv_branch(z, p["dw_conv_w"], p["dw_conv_b"], p["conv_ln_g"], p["conv_ln_b"], B)
    lru = functools.partial(_lru_pass, z, batch=B, col_x=2, col_y=1)
    hb = lru(None, p["rg_conv_w"][1], p["rg_conv_b"][1], p["rg_w_a"][1], p["rg_b_a"][1],
             p["rg_w_i"][1], p["rg_b_i"][1], p["rg_lam"][1], reverse=True)
    r = lru(hb, p["rg_conv_w"][0], p["rg_conv_b"][0], p["rg_w_a"][0], p["rg_b_a"][0],
            p["rg_w_i"][0], p["rg_b_i"][0], p["rg_lam"][0], reverse=False)
    return _outproj_e(c, r, x, p["w_out_e"])


def _trunks(xs, p):
    D = xs[0].shape[-1]
    lens = [x.shape[0] * x.shape[1] for x in xs]
    offs = [sum(lens[:k]) for k in range(len(xs))]
    T = sum(lens)

    h = _ffn([_even_mixer(x, p).reshape(n, D) for x, n in zip(xs, lens)],
             p["ln_ffn_e"], p["ffn_w1"], p["ffn_w3"], p["ffn_w2"])

    u, v = _inproj_o(h, p["ln_mix_o"], p["w_in_o"], p["sgu_ln_g"], p["sgu_ln_b"])
    h = _sgu_out(u, v, p["sgu_w"], p["sgu_b"], p["w_out_o"], h)

    E = p["w_router"].shape[1]
    tmx = MOE_ROW_TILE
    tf = MOE_FF_TILE if p["moe_w1"].shape[-1] % MOE_FF_TILE == 0 else _pick_tile(p["moe_w1"].shape[-1], 512)
    n_tiles = (2 * T) // tmx + E
    meta, cnt = _router(h, p["ln_ffn_o"], p["w_router"])
    plan = _route_plan(meta, cnt, E, tmx, n_tiles)
    xsort = _dispatch(h, p["ln_ffn_o"], plan, n_tiles * tmx, tmx)
    y = _moe_experts(xsort, plan["tile_expert"], plan["n_active"], p["moe_w1"], p["moe_w3"], p["moe_w2"],
                     tmx=tmx, tf=tf)
    return tuple(_combine(y, plan["pos"], meta, h, p["ln_final"], n, off).reshape(x.shape)
                 for x, n, off in zip(xs, lens, offs))


def kernel(x_prompt, x_sample, ln_mix_e, w_in_e, dw_conv_w, dw_conv_b, conv_ln_g, conv_ln_b, rg_conv_w, rg_conv_b, rg_w_a, rg_b_a, rg_w_i, rg_b_i, rg_lam, w_out_e, ln_ffn_e, ffn_w1, ffn_w3, ffn_w2, ln_mix_o, w_in_o, sgu_ln_g, sgu_ln_b, sgu_w, sgu_b, w_out_o, ln_ffn_o, w_router, moe_w1, moe_w3, moe_w2, ln_final):
    assert ln_mix_e.shape[0] == 1 and ln_mix_o.shape[0] == 1, "one even and one odd layer"
    bf = lambda a: a[0].astype(BF16)
    f32 = lambda a: a[0].astype(F32)
    p = {
        "ln_mix_e": f32(ln_mix_e), "w_in_e": bf(w_in_e),
        "dw_conv_w": f32(dw_conv_w), "dw_conv_b": f32(dw_conv_b),
        "conv_ln_g": f32(conv_ln_g), "conv_ln_b": f32(conv_ln_b),
        "rg_conv_w": f32(rg_conv_w), "rg_conv_b": f32(rg_conv_b),
        "rg_w_a": f32(rg_w_a), "rg_b_a": f32(rg_b_a), "rg_w_i": f32(rg_w_i), "rg_b_i": f32(rg_b_i),
        "rg_lam": f32(rg_lam), "w_out_e": bf(w_out_e),
        "ln_ffn_e": f32(ln_ffn_e), "ffn_w1": bf(ffn_w1), "ffn_w3": bf(ffn_w3), "ffn_w2": bf(ffn_w2),
        "ln_mix_o": f32(ln_mix_o), "w_in_o": bf(w_in_o),
        "sgu_ln_g": f32(sgu_ln_g), "sgu_ln_b": f32(sgu_ln_b),
        "sgu_w": bf(sgu_w), "sgu_b": f32(sgu_b), "w_out_o": bf(w_out_o),
        "ln_ffn_o": f32(ln_ffn_o), "w_router": f32(w_router),
        "moe_w1": bf(moe_w1), "moe_w3": bf(moe_w3), "moe_w2": bf(moe_w2),
        "ln_final": ln_final.astype(F32),
    }
    return _trunks([x_prompt, x_sample], p)
```

```python
import functools

import jax
import jax.numpy as jnp
from jax import lax
from jax.experimental import pallas as pl
from jax.experimental.pallas import tpu as pltpu

EPS = 1e-6
LRU_C = 8.0
LANES = 128
SUBLANES = 8
MXU_DIM = 256
VMEM_LIMIT_BYTES = 56 * 1024 * 1024
F32 = jnp.float32
BF16 = jnp.bfloat16


def _params(*sem):
    return pltpu.CompilerParams(dimension_semantics=sem, vmem_limit_bytes=VMEM_LIMIT_BYTES)


def _sigmoid(x):
    return 0.5 * (jnp.tanh(0.5 * x) + 1.0)


def _gelu_tanh(x):
    c = 0.7978845608028654
    hx = 0.5 * x
    return hx + hx * jnp.tanh(x * (c + (c * 0.044715) * (x * x)))


def _rmsnorm_f32(x, g):
    return x * lax.rsqrt(jnp.mean(x * x, axis=-1, keepdims=True) + EPS) * g


def _pick_tile(n, want):
    t = min(n, want)
    while n % t:
        t //= 2
    return t


def _perm_matrix(batch):
    tl = MXU_DIM // batch
    r = jnp.arange(MXU_DIM)
    src = (r % batch) * tl + r // batch
    return (src[:, None] == r[None, :]).astype(BF16)


def _inproj_e_kernel(x_ref, g_ref, p_ref, w_ref, o_ref, xt_ref, *, n_chunk):
    B, tt, _ = x_ref.shape
    tl = MXU_DIM // B
    for q in range(tt // tl):
        xb = jnp.concatenate([x_ref[b, q * tl:(q + 1) * tl, :] for b in range(B)], axis=0)
        xn = _rmsnorm_f32(xb, g_ref[...]).astype(BF16)
        xt_ref[q * MXU_DIM:(q + 1) * MXU_DIM, :] = jnp.dot(
            p_ref[...], xn, preferred_element_type=F32).astype(BF16)
    xt = xt_ref[...]
    n = o_ref.shape[-1]
    for j in range(n // n_chunk):
        cols = slice(j * n_chunk, (j + 1) * n_chunk)
        o_ref[:, cols] = jnp.dot(xt, w_ref[:, cols], preferred_element_type=F32).astype(o_ref.dtype)


def _inproj_e(x, g, w):
    B, S, D = x.shape
    N = w.shape[1]
    assert MXU_DIM % B == 0
    tt = _pick_tile(S, 128)
    assert tt % (MXU_DIM // B) == 0
    return pl.pallas_call(
        functools.partial(_inproj_e_kernel, n_chunk=1024),
        grid=(S // tt,),
        in_specs=[
            pl.BlockSpec((B, tt, D), lambda i: (0, i, 0)),
            pl.BlockSpec((1, D), lambda i: (0, 0)),
            pl.BlockSpec((MXU_DIM, MXU_DIM), lambda i: (0, 0)),
            pl.BlockSpec((D, N), lambda i: (0, 0)),
        ],
        out_specs=pl.BlockSpec((tt * B, N), lambda i: (i, 0)),
        out_shape=jax.ShapeDtypeStruct((S * B, N), BF16),
        scratch_shapes=[pltpu.VMEM((tt * B, D), BF16)],
        compiler_params=_params("parallel"),
        name="inproj_even",
    )(x, g.reshape(1, D), _perm_matrix(B), w)


def _conv_kernel(zp_ref, zm_ref, zn_ref, w_ref, b_ref, g_ref, be_ref, o_ref, s_ref, y_ref,
                 *, rows, halo, cdim, ktaps, batch, unroll, ln_rows, n_steps):
    i = pl.program_id(0)
    n = n_steps

    def glu(z):
        return z[:, :cdim].astype(F32) * _sigmoid(z[:, cdim:].astype(F32))

    s_ref[halo:halo + rows, :] = glu(zm_ref[...])
    s_ref[0:halo, :] = jnp.where(i > 0, glu(zp_ref[...]), 0.0)
    s_ref[halo + rows:, :] = jnp.where(i < n - 1, glu(zn_ref[...]), 0.0)

    pad = ktaps // 2
    base = halo - pad * batch
    step = batch * unroll
    for j in range(cdim // LANES):
        lanes = slice(j * LANES, (j + 1) * LANES)
        wj = [jnp.broadcast_to(w_ref[k:k + 1, lanes], (batch, LANES)) for k in range(ktaps)]
        bj = jnp.broadcast_to(b_ref[:, lanes], (batch, LANES))

        def body(ci, carry, lanes=lanes, wj=wj, bj=bj):
            r0 = pl.multiple_of(ci * step, step)
            accs = [bj] * unroll
            for m in range(unroll + ktaps - 1):
                xm = s_ref[pl.ds(r0 + base + batch * m, batch), lanes]
                for u in range(unroll):
                    k = m - u
                    if 0 <= k < ktaps:
                        accs[u] = accs[u] + wj[k] * xm
            for u in range(unroll):
                y_ref[pl.ds(r0 + batch * u, batch), lanes] = accs[u]
            return carry

        lax.fori_loop(0, rows // step, body, 0)

    def ln_body(ci, carry):
        r0 = pl.multiple_of(ci * ln_rows, ln_rows)
        y = y_ref[pl.ds(r0, ln_rows), :]
        mu = jnp.mean(y, axis=-1, keepdims=True)
        d = y - mu
        var = jnp.mean(d * d, axis=-1, keepdims=True)
        yn = d * lax.rsqrt(var + EPS) * g_ref[...] + be_ref[...]
        o_ref[pl.ds(r0, ln_rows), :] = (yn * _sigmoid(yn)).astype(o_ref.dtype)
        return carry

    lax.fori_loop(0, rows // ln_rows, ln_body, 0)


def _conv_branch(z, w, b, ln_g, ln_b, batch):
    R = z.shape[0]
    ktaps, cdim = w.shape
    halo = 128
    assert (ktaps // 2) * batch <= halo and batch == SUBLANES
    rows = _pick_tile(R, 2048)
    assert rows % halo == 0
    hb = rows // halo
    nh = R // halo
    vec = lambda a: a.reshape(1, cdim)
    return pl.pallas_call(
        functools.partial(_conv_kernel, rows=rows, halo=halo, cdim=cdim, ktaps=ktaps, batch=batch,
                          unroll=8, ln_rows=min(rows, 256), n_steps=R // rows),
        grid=(R // rows,),
        in_specs=[
            pl.BlockSpec((halo, 2 * cdim), lambda i: (jnp.maximum(i * hb - 1, 0), 0)),
            pl.BlockSpec((rows, 2 * cdim), lambda i: (i, 0)),
            pl.BlockSpec((halo, 2 * cdim), lambda i: (jnp.minimum((i + 1) * hb, nh - 1), 0)),
            pl.BlockSpec((ktaps, cdim), lambda i: (0, 0)),
            pl.BlockSpec((1, cdim), lambda i: (0, 0)),
            pl.BlockSpec((1, cdim), lambda i: (0, 0)),
            pl.BlockSpec((1, cdim), lambda i: (0, 0)),
        ],
        out_specs=pl.BlockSpec((rows, cdim), lambda i: (i, 0)),
        out_shape=jax.ShapeDtypeStruct((R, cdim), BF16),
        scratch_shapes=[pltpu.VMEM((rows + 2 * halo, cdim), F32), pltpu.VMEM((rows, cdim), F32)],
        compiler_params=_params("parallel"),
        name="conv_branch",
    )(z, z, z, w, vec(b), vec(ln_g), vec(ln_b))


def _lru_kernel(*refs, rows, batch, ktaps, sub, reverse, final):
    if final:
        (zx_ref, zy_ref, hb_ref, cw_ref, cb_ref, wg_ref, ba_ref, bi_ref, lam_ref,
         o_ref, xs_ref, a_ref, u_ref, h_ref) = refs
    else:
        (zx_ref, cw_ref, cb_ref, wg_ref, ba_ref, bi_ref, lam_ref,
         o_ref, xs_ref, a_ref, u_ref, h_ref) = refs
    hl = (ktaps - 1) * batch
    main = 0 if reverse else hl
    ldim = zx_ref.shape[-1]
    ngrp = wg_ref.shape[0]
    gw = ldim // ngrp

    @pl.when(pl.program_id(0) == 0)
    def _():
        h_ref[...] = jnp.zeros_like(h_ref)
        xs_ref[...] = jnp.zeros_like(xs_ref)

    xs_ref[main:main + rows, :] = zx_ref[...].astype(F32)

    lam = lam_ref[...]
    sp = jnp.maximum(-lam, 0.0) + jnp.log(1.0 + jnp.exp(-jnp.abs(lam)))
    half_neg_c_sp = -0.5 * LRU_C * sp

    def gate_body(ci, carry):
        r0 = pl.multiple_of(ci * sub, sub)
        xc = jnp.broadcast_to(cb_ref[...], (sub, ldim))
        for k in range(ktaps):
            xc = xc + cw_ref[k:k + 1, :] * xs_ref[pl.ds(r0 + batch * k, sub), :]
        xcb = xc.astype(BF16)
        for g in range(ngrp):
            cols = slice(g * gw, (g + 1) * gw)
            gates = jnp.dot(xcb[:, cols], wg_ref[g], preferred_element_type=F32)
            tr = jnp.tanh(gates[:, :gw] + ba_ref[:, cols])
            ig2 = jnp.tanh(gates[:, gw:] + bi_ref[:, cols]) + 1.0
            log_a = (tr + 1.0) * half_neg_c_sp[:, cols]
            a_ref[pl.ds(r0, sub), cols] = jnp.exp(log_a)
            th = jnp.tanh(log_a)
            half_mult = jnp.sqrt(jnp.maximum(0.5 * th / (th - 1.0), 0.25e-12))
            u_ref[pl.ds(r0, sub), cols] = half_mult * (ig2 * xc[:, cols])
        return carry

    lax.fori_loop(0, rows // sub, gate_body, 0)

    if reverse:
        xs_ref[rows:rows + hl, :] = xs_ref[0:hl, :]
    else:
        xs_ref[0:hl, :] = xs_ref[rows:rows + hl, :]

    nt = rows // batch

    def scan_body(t, h):
        tt = (nt - 1 - t) if reverse else t
        r = pl.multiple_of(tt * batch, batch)
        h = a_ref[pl.ds(r, batch), :] * h + u_ref[pl.ds(r, batch), :]
        u_ref[pl.ds(r, batch), :] = h
        return h

    h_ref[...] = lax.fori_loop(0, nt, scan_body, h_ref[...], unroll=8)

    def out_body(ci, carry):
        r0 = pl.multiple_of(ci * sub, sub)
        h = u_ref[pl.ds(r0, sub), :]
        if final:
            h = h + hb_ref[pl.ds(r0, sub), :].astype(F32)
            h = _gelu_tanh(zy_ref[pl.ds(r0, sub), :].astype(F32)) * h
        o_ref[pl.ds(r0, sub), :] = h.astype(o_ref.dtype)
        return carry

    lax.fori_loop(0, rows // sub, out_body, 0)


def _pack_blockdiag(w, width):
    H, dh, _ = w.shape
    per = width // dh
    G = H // per
    w = w.reshape(G, per, dh, dh)
    eye = jnp.eye(per, dtype=w.dtype)
    return jnp.einsum("pq,gpij->gpiqj", eye, w).reshape(G, per * dh, per * dh)


def _lru_pass(z, hb, cw, cb, w_a, b_a, w_i, b_i, lam, *, batch, reverse, col_x, col_y):
    R = z.shape[0]
    ktaps, ldim = cw.shape
    final = hb is not None
    rows = _pick_tile(R, 1024)
    nblk = R // rows
    wg = (0.5 * jnp.concatenate([_pack_blockdiag(w_a, MXU_DIM), _pack_blockdiag(w_i, MXU_DIM)], axis=-1)).astype(BF16)
    b_a, b_i = 0.5 * b_a, 0.5 * b_i
    ngrp = wg.shape[0]
    vec = lambda a: a.reshape(1, ldim).astype(F32)
    blk = (lambda i: nblk - 1 - i) if reverse else (lambda i: i)
    const2 = lambda i: (0, 0)
    in_specs = [pl.BlockSpec((rows, ldim), lambda i: (blk(i), col_x))]
    args = [z]
    if final:
        in_specs += [pl.BlockSpec((rows, ldim), lambda i: (blk(i), col_y)),
                     pl.BlockSpec((rows, ldim), lambda i: (blk(i), 0))]
        args += [z, hb]
    in_specs += [
        pl.BlockSpec((ktaps, ldim), const2),
        pl.BlockSpec((1, ldim), const2),
        pl.BlockSpec((ngrp, MXU_DIM, 2 * MXU_DIM), lambda i: (0, 0, 0)),
        pl.BlockSpec((1, ldim), const2),
        pl.BlockSpec((1, ldim), const2),
        pl.BlockSpec((1, ldim), const2),
    ]
    args += [cw, vec(cb), wg, vec(b_a), vec(b_i), vec(lam)]
    hl = (ktaps - 1) * batch
    return pl.pallas_call(
        functools.partial(_lru_kernel, rows=rows, batch=batch, ktaps=ktaps, sub=min(rows, 256),
                          reverse=reverse, final=final),
        grid=(nblk,),
        in_specs=in_specs,
        out_specs=pl.BlockSpec((rows, ldim), lambda i: (blk(i), 0)),
        out_shape=jax.ShapeDtypeStruct((R, ldim), BF16),
        scratch_shapes=[
            pltpu.VMEM((rows + hl, ldim), F32),
            pltpu.VMEM((rows, ldim), F32),
            pltpu.VMEM((rows, ldim), F32),
            pltpu.VMEM((batch, ldim), F32),
        ],
        compiler_params=_params("arbitrary"),
        name="rglru_bwd" if reverse else "rglru_fwd",
    )(*args)


def _outproj_e_kernel(c_ref, r_ref, x_ref, pt_ref, w_ref, o_ref, m_ref):
    B, tt, _ = x_ref.shape
    tl = MXU_DIM // B
    cdim = c_ref.shape[1]
    for q in range(tt // tl):
        rows = slice(q * MXU_DIM, (q + 1) * MXU_DIM)
        m_ref[rows, :cdim] = jnp.dot(pt_ref[...], c_ref[rows, :], preferred_element_type=F32).astype(BF16)
        m_ref[rows, cdim:] = jnp.dot(pt_ref[...], r_ref[rows, :], preferred_element_type=F32).astype(BF16)
    acc = jnp.dot(m_ref[...], w_ref[...], preferred_element_type=F32)
    for q in range(tt // tl):
        for b in range(B):
            r0 = q * MXU_DIM + b * tl
            ts = slice(q * tl, (q + 1) * tl)
            o_ref[b, ts, :] = x_ref[b, ts, :] + acc[r0:r0 + tl, :]


def _outproj_e(c, r, x, w):
    B, S, D = x.shape
    cdim, ldim = c.shape[1], r.shape[1]
    tt = _pick_tile(S, 128)
    return pl.pallas_call(
        _outproj_e_kernel,
        grid=(S // tt,),
        in_specs=[
            pl.BlockSpec((tt * B, cdim), lambda i: (i, 0)),
            pl.BlockSpec((tt * B, ldim), lambda i: (i, 0)),
            pl.BlockSpec((B, tt, D), lambda i: (0, i, 0)),
            pl.BlockSpec((MXU_DIM, MXU_DIM), lambda i: (0, 0)),
            pl.BlockSpec((cdim + ldim, D), lambda i: (0, 0)),
        ],
        out_specs=pl.BlockSpec((B, tt, D), lambda i: (0, i, 0)),
        out_shape=jax.ShapeDtypeStruct((B, S, D), F32),
        scratch_shapes=[pltpu.VMEM((tt * B, cdim + ldim), BF16)],
        compiler_params=_params("parallel"),
        name="outproj_even",
    )(c, r, x, _perm_matrix(B).T, w)


FFN_COL_CHUNK = 2 * MXU_DIM


def _ffn_kernel(*refs, bounds):
    n_src = len(bounds) - 1
    x_refs = refs[:n_src]
    g_ref, w1_ref, w3_ref, w2_ref, o_ref, xn_ref = refs[n_src:]
    i = pl.program_id(0)
    for x_ref, lo, hi in zip(x_refs, bounds[:-1], bounds[1:]):
        @pl.when((i >= lo) & (i < hi))
        def _(x_ref=x_ref):
            x = x_ref[...]
            xn_ref[...] = _rmsnorm_f32(x, g_ref[...]).astype(BF16)
            o_ref[...] = x

    xn = xn_ref[...]
    F = w1_ref.shape[1]
    acc = None
    for c0 in range(0, F, FFN_COL_CHUNK):
        c1 = min(c0 + FFN_COL_CHUNK, F)
        a = jnp.dot(xn, w1_ref[:, c0:c1], preferred_element_type=F32)
        b = jnp.dot(xn, w3_ref[:, c0:c1], preferred_element_type=F32)
        act = (a * _sigmoid(a) * b).astype(BF16)
        part = jnp.dot(act, w2_ref[c0:c1, :], preferred_element_type=F32)
        acc = part if acc is None else acc + part
    o_ref[...] += acc


def _ffn(xs, g, w1, w3, w2, *, tm=1024):
    D = xs[0].shape[1]
    F = w1.shape[1]
    assert all(x.shape[0] % tm == 0 for x in xs)
    counts = [x.shape[0] // tm for x in xs]
    starts = [sum(counts[:k]) for k in range(len(xs))]
    T = tm * sum(counts)

    def src_spec(lo, n):
        return pl.BlockSpec((tm, D), lambda i: (jnp.clip(i - lo, 0, n - 1), 0))

    resident = dict(pipeline_mode=pl.Buffered(1))
    return pl.pallas_call(
        functools.partial(_ffn_kernel, bounds=tuple(starts) + (T // tm,)),
        grid=(T // tm,),
        in_specs=[src_spec(lo, n) for lo, n in zip(starts, counts)] + [
            pl.BlockSpec((1, D), lambda i: (0, 0)),
            pl.BlockSpec((D, F), lambda i: (0, 0), **resident),
            pl.BlockSpec((D, F), lambda i: (0, 0), **resident),
            pl.BlockSpec((F, D), lambda i: (0, 0), **resident),
        ],
        out_specs=pl.BlockSpec((tm, D), lambda i: (i, 0)),
        out_shape=jax.ShapeDtypeStruct((T, D), F32),
        scratch_shapes=[pltpu.VMEM((tm, D), BF16)],
        compiler_params=_params("parallel"),
        name="ffn_swiglu",
    )(*xs, g.reshape(1, D), w1, w3, w2)


def _inproj_o_kernel(x_ref, g_ref, w_ref, lg_ref, lb_ref, u_ref, v_ref, zv_ref, *, sdim, n_chunk):
    xn = _rmsnorm_f32(x_ref[...], g_ref[...]).astype(BF16)
    for j in range(sdim // n_chunk):
        cols = slice(j * n_chunk, (j + 1) * n_chunk)
        zu = _gelu_tanh(jnp.dot(xn, w_ref[:, cols], preferred_element_type=F32))
        u_ref[:, cols] = zu.astype(u_ref.dtype)
    for j in range(sdim // n_chunk):
        cols = slice(j * n_chunk, (j + 1) * n_chunk)
        wcols = slice(sdim + j * n_chunk, sdim + (j + 1) * n_chunk)
        zv_ref[:, cols] = _gelu_tanh(jnp.dot(xn, w_ref[:, wcols], preferred_element_type=F32))
    zv = zv_ref[...]
    mu = jnp.mean(zv, axis=-1, keepdims=True)
    d = zv - mu
    var = jnp.mean(d * d, axis=-1, keepdims=True)
    v_ref[...] = (d * lax.rsqrt(var + EPS) * lg_ref[...] + lb_ref[...]).astype(v_ref.dtype)


def _inproj_o(x, g, w, ln_g, ln_b):
    T, D = x.shape
    sdim = w.shape[1] // 2
    tm = _pick_tile(T, 512)
    row = lambda i: (i, 0)
    const = lambda i: (0, 0)
    return pl.pallas_call(
        functools.partial(_inproj_o_kernel, sdim=sdim, n_chunk=_pick_tile(sdim, 512)),
        grid=(T // tm,),
        scratch_shapes=[pltpu.VMEM((tm, sdim), F32)],
        in_specs=[
            pl.BlockSpec((tm, D), row),
            pl.BlockSpec((1, D), const),
            pl.BlockSpec((D, 2 * sdim), const),
            pl.BlockSpec((1, sdim), const),
            pl.BlockSpec((1, sdim), const),
        ],
        out_specs=[pl.BlockSpec((tm, sdim), row), pl.BlockSpec((tm, sdim), row)],
        out_shape=[jax.ShapeDtypeStruct((T, sdim), BF16), jax.ShapeDtypeStruct((T, sdim), BF16)],
        compiler_params=_params("parallel"),
        name="inproj_odd",
    )(x, g.reshape(1, D), w, ln_g.reshape(1, sdim), ln_b.reshape(1, sdim))


def _sgu_out_kernel(u_ref, v_ref, sw_ref, sb_ref, w_ref, x_ref, o_ref, g_ref, *, chunk, heads):
    tm, sdim = u_ref.shape
    hd = sdim // heads
    for n in range(tm // chunk):
        rows = slice(n * chunk, (n + 1) * chunk)
        for h in range(heads):
            lanes = slice(h * hd, (h + 1) * hd)
            sv = jnp.dot(sw_ref[h], v_ref[rows, lanes], preferred_element_type=F32) + sb_ref[h]
            g_ref[rows, lanes] = (u_ref[rows, lanes].astype(F32) * sv).astype(BF16)
    o_ref[...] = x_ref[...] + jnp.dot(g_ref[...], w_ref[...], preferred_element_type=F32)


def _sgu_out(u, v, sgu_w, sgu_b, w, x):
    T, sdim = u.shape
    D = x.shape[1]
    heads, chunk, _ = sgu_w.shape
    hd = sdim // heads
    tm = _pick_tile(T, 512)
    assert tm % chunk == 0
    sb = jnp.broadcast_to(sgu_b.astype(F32)[:, :, None], (heads, chunk, hd))
    row = lambda i: (i, 0)
    return pl.pallas_call(
        functools.partial(_sgu_out_kernel, chunk=chunk, heads=heads),
        grid=(T // tm,),
        in_specs=[
            pl.BlockSpec((tm, sdim), row),
            pl.BlockSpec((tm, sdim), row),
            pl.BlockSpec((heads, chunk, chunk), lambda i: (0, 0, 0)),
            pl.BlockSpec((heads, chunk, hd), lambda i: (0, 0, 0)),
            pl.BlockSpec((sdim, D), lambda i: (0, 0)),
            pl.BlockSpec((tm, D), row),
        ],
        out_specs=pl.BlockSpec((tm, D), row),
        out_shape=jax.ShapeDtypeStruct((T, D), F32),
        scratch_shapes=[pltpu.VMEM((tm, sdim), BF16)],
        compiler_params=_params("parallel"),
        name="sgu_outproj_odd",
    )(u, v, sgu_w, sb, w, x)


META_E1, META_E2, META_R1, META_R2, META_G1, META_G2 = range(6)
ROUTE_TILE = 512


def _router_kernel(x_ref, g_ref, wr_ref, meta_ref, cnt_ref, run_ref, *, n_experts):
    @pl.when(pl.program_id(0) == 0)
    def _():
        run_ref[...] = jnp.zeros_like(run_ref)

    xn = _rmsnorm_f32(x_ref[...], g_ref[...])
    xh = xn.astype(BF16)
    xl = (xn - xh.astype(F32)).astype(BF16)
    logits = (jnp.dot(xh, wr_ref[0], preferred_element_type=F32)
              + jnp.dot(xl, wr_ref[0], preferred_element_type=F32)
              + jnp.dot(xh, wr_ref[1], preferred_element_type=F32))
    tm = logits.shape[0]
    lane = lax.broadcasted_iota(jnp.int32, logits.shape, 1).astype(F32)
    neg = jnp.float32(-jnp.inf)
    big = jnp.float32(1e9)
    l1 = jnp.where(lane < n_experts, logits, neg)
    m1 = jnp.max(l1, axis=-1, keepdims=True)
    i1 = jnp.min(jnp.where(l1 == m1, lane, big), axis=-1, keepdims=True)
    l2 = jnp.where(lane == i1, neg, l1)
    m2 = jnp.max(l2, axis=-1, keepdims=True)
    i2 = jnp.min(jnp.where(l2 == m2, lane, big), axis=-1, keepdims=True)
    e2 = jnp.exp(m2 - m1)
    den = 1.0 + e2
    g1 = 1.0 / den
    g2 = e2 / den

    oh1 = jnp.where(lane == i1, 1.0, 0.0)
    oh2 = jnp.where(lane == i2, 1.0, 0.0)
    cnt = oh1 + oh2
    ri = lax.broadcasted_iota(jnp.int32, (tm, tm), 0)
    ci = lax.broadcasted_iota(jnp.int32, (tm, tm), 1)
    lower = jnp.where(ci < ri, 1.0, 0.0).astype(BF16)
    prefix = jnp.dot(lower, cnt.astype(BF16), preferred_element_type=F32) + run_ref[...]
    r1 = jnp.sum(oh1 * prefix, axis=-1, keepdims=True)
    r2 = jnp.sum(oh2 * prefix, axis=-1, keepdims=True)
    run_ref[...] += jnp.sum(cnt, axis=0, keepdims=True)

    meta = jnp.zeros_like(logits)
    for idx, val in ((META_E1, i1), (META_E2, i2), (META_R1, r1), (META_R2, r2), (META_G1, g1), (META_G2, g2)):
        meta = jnp.where(lane == idx, val, meta)
    meta_ref[...] = meta
    cnt_ref[...] = jnp.broadcast_to(run_ref[...], cnt_ref.shape)


def _router(x, g, w_router):
    T, D = x.shape
    E = w_router.shape[1]
    wr = jnp.zeros((D, LANES), F32).at[:, :E].set(w_router.astype(F32))
    wr_hi = wr.astype(BF16)
    wr = jnp.stack([wr_hi, (wr - wr_hi.astype(F32)).astype(BF16)])
    tm = ROUTE_TILE
    assert T % tm == 0
    row = lambda i: (i, 0)
    return pl.pallas_call(
        functools.partial(_router_kernel, n_experts=E),
        grid=(T // tm,),
        in_specs=[
            pl.BlockSpec((tm, D), row),
            pl.BlockSpec((1, D), lambda i: (0, 0)),
            pl.BlockSpec((2, D, LANES), lambda i: (0, 0, 0)),
        ],
        out_specs=[pl.BlockSpec((tm, LANES), row), pl.BlockSpec((SUBLANES, LANES), lambda i: (0, 0))],
        out_shape=[jax.ShapeDtypeStruct((T, LANES), F32), jax.ShapeDtypeStruct((SUBLANES, LANES), F32)],
        scratch_shapes=[pltpu.VMEM((1, LANES), F32)],
        compiler_params=_params("arbitrary"),
        name="moe_router",
    )(x, g.reshape(1, D), wr)


def _route_plan(meta, cnt, n_experts, tmx, n_tiles):
    counts = cnt[0, :n_experts].astype(jnp.int32)
    nt = (counts + tmx - 1) // tmx
    tend = jnp.cumsum(nt)
    off = (tend - nt) * tmx
    eids = jnp.arange(n_experts, dtype=jnp.int32)

    def pos(e_lane, r_lane):
        e = meta[:, e_lane].astype(jnp.int32)
        base = jnp.sum(jnp.where(e[:, None] == eids[None, :], off[None, :], 0), axis=1)
        return base + meta[:, r_lane].astype(jnp.int32)

    T = meta.shape[0]
    tm = ROUTE_TILE
    p1 = pos(META_E1, META_R1).reshape(T // tm, 1, tm)
    p2 = pos(META_E2, META_R2).reshape(T // tm, 1, tm)
    tiles = jnp.arange(n_tiles, dtype=jnp.int32)
    tile_expert = jnp.minimum(jnp.sum((tend[None, :] <= tiles[:, None]).astype(jnp.int32), axis=1), n_experts - 1)
    plan = dict(pos=jnp.concatenate([p1, p2], axis=2), tile_expert=tile_expert.astype(jnp.int32),
                n_active=tend[-1:].astype(jnp.int32), pad_start=(off + counts).astype(jnp.int32),
                pad_len=(nt * tmx - counts).astype(jnp.int32))
    return plan


def _dispatch_kernel(pstart_ref, plen_ref, na_ref, pos_ref, x_ref, g_ref, w1_ref, w3_ref, w2_ref,
                     xs_ref, w1o_ref, w3o_ref, w2o_ref, xn_ref, z_ref, sem, zsem,
                     *, pad_bits, n_steps, tmx):
    i = pl.program_id(0)
    tm = x_ref.shape[0]
    slot = i % 2

    third = n_steps // 3
    for k, (src_ref, dst_ref) in enumerate(((w1_ref, w1o_ref), (w3_ref, w3o_ref), (w2_ref, w2o_ref))):
        @pl.when((i >= k * third) & (i < (k + 1) * third))
        def _(src_ref=src_ref, dst_ref=dst_ref):
            dst_ref[...] = src_ref[...].astype(dst_ref.dtype)

    @pl.when(i == 0)
    def _():
        z_ref[...] = jnp.zeros_like(z_ref)

        def zero_rows(dst, rows):
            cp = pltpu.make_async_copy(z_ref.at[pl.ds(0, rows)], xs_ref.at[pl.ds(dst, rows)], zsem)
            cp.start()
            cp.wait()

        for e in range(pstart_ref.shape[0]):
            n = plen_ref[e]
            s = pstart_ref[e]
            head = n & (SUBLANES - 1)
            for j in range(SUBLANES - 1):
                @pl.when(j < head)
                def _(j=j, s=s):
                    zero_rows(s + j, 1)

            groups = n >> 3
            base = s + head
            for k in reversed(range(pad_bits - 3)):
                @pl.when(((groups >> k) & 1) == 1)
                def _(k=k, groups=groups, base=base):
                    dst = pl.multiple_of(base + (((groups >> (k + 1)) << (k + 1)) << 3), SUBLANES)
                    zero_rows(dst, SUBLANES << k)

        zrows = z_ref.shape[0]
        n_tiles = xs_ref.shape[0] // tmx
        for t in range(n_tiles - pstart_ref.shape[0], n_tiles):
            @pl.when(t >= na_ref[0])
            def _(t=t):
                for c in range(tmx // zrows):
                    zero_rows(t * tmx + c * zrows, zrows)

    xn = _rmsnorm_f32(x_ref[...], g_ref[...])

    for s in range(2):
        @pl.when(slot == s)
        def _(s=s):
            xn_ref[s] = xn

            def issue(q, carry):
                r0 = pl.multiple_of(q * SUBLANES, SUBLANES)
                for j in range(SUBLANES):
                    src = xn_ref.at[s, pl.ds(r0 + j, 1)]
                    pltpu.make_async_copy(src, xs_ref.at[pl.ds(pos_ref[0, r0 + j], 1)], sem.at[s]).start()
                    pltpu.make_async_copy(src, xs_ref.at[pl.ds(pos_ref[0, tm + r0 + j], 1)], sem.at[s]).start()
                return carry

            lax.fori_loop(0, tm // SUBLANES, issue, 0)

    def drain(s):
        for _ in range(2):
            pltpu.make_async_copy(xn_ref.at[s], xs_ref.at[pl.ds(0, tm)], sem.at[s]).wait()

    @pl.when(i > 0)
    def _():
        drain(1 - slot)

    @pl.when(i == n_steps - 1)
    def _():
        drain(slot)


def _dispatch(x, g, plan, n_rows, tmx, w1, w3, w2):
    T, D = x.shape
    tm = ROUTE_TILE
    n = T // tm
    pad_bits = (tmx - 1).bit_length()
    assert n_rows % tmx == 0 and tmx % (1 << (pad_bits - 1)) == 0
    E, _, F = w1.shape
    third = n // 3
    assert n % 3 == 0 and (E * D) % third == 0 and (E * F) % third == 0
    r13, r2 = (E * D) // third, (E * F) // third
    assert r13 % (2 * SUBLANES) == 0 and r2 % (2 * SUBLANES) == 0

    def slab(rows, width, k):
        return pl.BlockSpec((rows, width), lambda i, ps, pn, na: (jnp.clip(i - k * third, 0, third - 1), 0))

    grid_spec = pltpu.PrefetchScalarGridSpec(
        num_scalar_prefetch=3,
        grid=(n,),
        in_specs=[
            pl.BlockSpec((None, 1, 2 * tm), lambda i, ps, pn, na: (i, 0, 0), memory_space=pltpu.SMEM),
            pl.BlockSpec((tm, D), lambda i, ps, pn, na: (i, 0)),
            pl.BlockSpec((1, D), lambda i, ps, pn, na: (0, 0)),
            slab(r13, F, 0), slab(r13, F, 1), slab(r2, D, 2),
        ],
        out_specs=[pl.BlockSpec(memory_space=pl.ANY), slab(r13, F, 0), slab(r13, F, 1), slab(r2, D, 2)],
        scratch_shapes=[
            pltpu.VMEM((2, tm, D), F32),
            pltpu.VMEM((1 << (pad_bits - 1), D), F32),
            pltpu.SemaphoreType.DMA((2,)),
            pltpu.SemaphoreType.DMA(()),
        ],
    )
    xs, w1b, w3b, w2b = pl.pallas_call(
        functools.partial(_dispatch_kernel, pad_bits=pad_bits, n_steps=n, tmx=tmx),
        grid_spec=grid_spec,
        out_shape=[jax.ShapeDtypeStruct((n_rows, D), F32), jax.ShapeDtypeStruct((E * D, F), BF16),
                   jax.ShapeDtypeStruct((E * D, F), BF16), jax.ShapeDtypeStruct((E * F, D), BF16)],
        compiler_params=_params("arbitrary"),
        name="moe_dispatch",
    )(plan["pad_start"], plan["pad_len"], plan["n_active"], plan["pos"], x, g.reshape(1, D),
      w1.reshape(E * D, F), w3.reshape(E * D, F), w2.reshape(E * F, D))
    return xs, w1b.reshape(E, D, F), w3b.reshape(E, D, F), w2b.reshape(E, F, D)


MOE_COL_CHUNK = 4 * MXU_DIM


def _moe_experts_kernel(te_ref, na_ref, x_ref, w1_ref, w3_ref, w2_ref, o_ref, xb_ref):
    del te_ref
    f = pl.program_id(1)
    active = pl.program_id(0) < na_ref[0]

    @pl.when(f == 0)
    def _():
        o_ref[...] = jnp.zeros_like(o_ref)

    @pl.when((f == 0) & active)
    def _():
        xb_ref[...] = x_ref[...].astype(BF16)

    @pl.when(active)
    def _():
        xn = xb_ref[...]
        tf = w1_ref.shape[1]
        acc = None
        for c0 in range(0, tf, MOE_COL_CHUNK):
            c1 = min(c0 + MOE_COL_CHUNK, tf)
            a = jnp.dot(xn, w1_ref[:, c0:c1], preferred_element_type=F32)
            b = jnp.dot(xn, w3_ref[:, c0:c1], preferred_element_type=F32)
            act = (a * _sigmoid(a) * b).astype(BF16)
            part = jnp.dot(act, w2_ref[c0:c1, :], preferred_element_type=F32)
            acc = part if acc is None else acc + part
        o_ref[...] += acc


def _moe_experts(xs, tile_expert, n_active, w1, w3, w2, *, tmx, tf):
    n_rows, D = xs.shape
    E, _, F = w1.shape
    assert F % tf == 0 and n_rows % tmx == 0
    nf = F // tf
    fsel = lambda i, f, na: jnp.where(i < na[0], f, nf - 1)
    grid_spec = pltpu.PrefetchScalarGridSpec(
        num_scalar_prefetch=2,
        grid=(n_rows // tmx, nf),
        in_specs=[
            pl.BlockSpec((tmx, D), lambda i, f, te, na: (jnp.minimum(i, na[0] - 1), 0)),
            pl.BlockSpec((None, D, tf), lambda i, f, te, na: (te[i], 0, fsel(i, f, na))),
            pl.BlockSpec((None, D, tf), lambda i, f, te, na: (te[i], 0, fsel(i, f, na))),
            pl.BlockSpec((None, tf, D), lambda i, f, te, na: (te[i], fsel(i, f, na), 0)),
        ],
        out_specs=pl.BlockSpec((tmx, D), lambda i, f, te, na: (i, 0)),
        scratch_shapes=[pltpu.VMEM((tmx, D), BF16)],
    )
    return pl.pallas_call(
        _moe_experts_kernel,
        grid_spec=grid_spec,
        out_shape=jax.ShapeDtypeStruct((n_rows, D), F32),
        compiler_params=_params("parallel", "arbitrary"),
        name="moe_experts",
    )(tile_expert, n_active, xs, w1, w3, w2)


def _combine_kernel(pos_ref, posn_ref, meta_ref, res_ref, lnf_ref, y_ref, o_ref, ya_ref, yb_ref, sem, *, n_steps):
    i = pl.program_id(0)
    tm = res_ref.shape[0]
    slot = i % 2

    def gather(p_ref, s):
        def issue(q, carry):
            r0 = pl.multiple_of(q * SUBLANES, SUBLANES)
            for j in range(SUBLANES):
                r = r0 + j
                pltpu.make_async_copy(y_ref.at[pl.ds(p_ref[0, r], 1)], ya_ref.at[s, pl.ds(r, 1)], sem.at[s]).start()
                pltpu.make_async_copy(y_ref.at[pl.ds(p_ref[0, tm + r], 1)], yb_ref.at[s, pl.ds(r, 1)],
                                      sem.at[s]).start()
            return carry

        lax.fori_loop(0, tm // SUBLANES, issue, 0)

    @pl.when(i == 0)
    def _():
        gather(pos_ref, 0)

    for s in range(2):
        @pl.when((i + 1 < n_steps) & (slot == 1 - s))
        def _(s=s):
            gather(posn_ref, s)

    pltpu.make_async_copy(y_ref.at[pl.ds(0, tm)], ya_ref.at[slot], sem.at[slot]).wait()
    pltpu.make_async_copy(y_ref.at[pl.ds(0, tm)], yb_ref.at[slot], sem.at[slot]).wait()

    meta = meta_ref[...]
    g1 = meta[:, META_G1:META_G1 + 1]
    g2 = meta[:, META_G2:META_G2 + 1]
    h = res_ref[...] + g1 * ya_ref[slot] + g2 * yb_ref[slot]
    o_ref[...] = _rmsnorm_f32(h, lnf_ref[...])


def _combine(y, pos, meta, res, ln_final, t_len, t_off):
    D = res.shape[1]
    tm = ROUTE_TILE
    assert t_len % tm == 0 and t_off % tm == 0
    off = t_off // tm
    last = off + t_len // tm - 1
    return pl.pallas_call(
        functools.partial(_combine_kernel, n_steps=t_len // tm),
        grid=(t_len // tm,),
        in_specs=[
            pl.BlockSpec((None, 1, 2 * tm), lambda i: (off + i, 0, 0), memory_space=pltpu.SMEM),
            pl.BlockSpec((None, 1, 2 * tm), lambda i: (jnp.minimum(off + i + 1, last), 0, 0),
                         memory_space=pltpu.SMEM),
            pl.BlockSpec((tm, LANES), lambda i: (off + i, 0)),
            pl.BlockSpec((tm, D), lambda i: (off + i, 0)),
            pl.BlockSpec((1, D), lambda i: (0, 0)),
            pl.BlockSpec(memory_space=pl.ANY),
        ],
        out_specs=pl.BlockSpec((tm, D), lambda i: (i, 0)),
        out_shape=jax.ShapeDtypeStruct((t_len, D), F32),
        scratch_shapes=[pltpu.VMEM((2, tm, D), F32), pltpu.VMEM((2, tm, D), F32), pltpu.SemaphoreType.DMA((2,))],
        compiler_params=_params("arbitrary"),
        name="moe_combine",
    )(pos, pos, meta, res, ln_final.reshape(1, D), y)


MOE_ROW_TILE = 1024
MOE_FF_TILE = 1792


def _even_mixer(x, p):
    B = x.shape[0]
    cdim = p["dw_conv_w"].shape[-1]
    ldim = p["rg_lam"].shape[-1]
    assert cdim * 2 == ldim and p["w_in_e"].shape[-1] == 2 * cdim + 2 * ldim
    z = _inproj_e(x, p["ln_mix_e"], p["w_in_e"])
    c = _conv_branch(z, p["dw_conv_w"], p["dw_conv_b"], p["conv_ln_g"], p["conv_ln_b"], B)
    lru = functools.partial(_lru_pass, z, batch=B, col_x=2, col_y=1)
    hb = lru(None, p["rg_conv_w"][1], p["rg_conv_b"][1], p["rg_w_a"][1], p["rg_b_a"][1],
             p["rg_w_i"][1], p["rg_b_i"][1], p["rg_lam"][1], reverse=True)
    r = lru(hb, p["rg_conv_w"][0], p["rg_conv_b"][0], p["rg_w_a"][0], p["rg_b_a"][0],
            p["rg_w_i"][0], p["rg_b_i"][0], p["rg_lam"][0], reverse=False)
    return _outproj_e(c, r, x, p["w_out_e"])


def _trunks(xs, p):
    D = xs[0].shape[-1]
    lens = [x.shape[0] * x.shape[1] for x in xs]
    offs = [sum(lens[:k]) for k in range(len(xs))]
    T = sum(lens)

    h = _ffn([_even_mixer(x, p).reshape(n, D) for x, n in zip(xs, lens)],
             p["ln_ffn_e"], p["ffn_w1"], p["ffn_w3"], p["ffn_w2"])

    u, v = _inproj_o(h, p["ln_mix_o"], p["w_in_o"], p["sgu_ln_g"], p["sgu_ln_b"])
    h = _sgu_out(u, v, p["sgu_w"], p["sgu_b"], p["w_out_o"], h)

    E = p["w_router"].shape[1]
    tmx = MOE_ROW_TILE
    tf = MOE_FF_TILE if p["moe_w1"].shape[-1] % MOE_FF_TILE == 0 else _pick_tile(p["moe_w1"].shape[-1], 512)
    n_tiles = (2 * T) // tmx + E
    meta, cnt = _router(h, p["ln_ffn_o"], p["w_router"])
    plan = _route_plan(meta, cnt, E, tmx, n_tiles)
    xsort, w1, w3, w2 = _dispatch(h, p["ln_ffn_o"], plan, n_tiles * tmx, tmx,
                                  p["moe_w1"], p["moe_w3"], p["moe_w2"])
    y = _moe_experts(xsort, plan["tile_expert"], plan["n_active"], w1, w3, w2, tmx=tmx, tf=tf)
    return tuple(_combine(y, plan["pos"], meta, h, p["ln_final"], n, off).reshape(x.shape)
                 for x, n, off in zip(xs, lens, offs))


def kernel(x_prompt, x_sample, ln_mix_e, w_in_e, dw_conv_w, dw_conv_b, conv_ln_g, conv_ln_b, rg_conv_w, rg_conv_b, rg_w_a, rg_b_a, rg_w_i, rg_b_i, rg_lam, w_out_e, ln_ffn_e, ffn_w1, ffn_w3, ffn_w2, ln_mix_o, w_in_o, sgu_ln_g, sgu_ln_b, sgu_w, sgu_b, w_out_o, ln_ffn_o, w_router, moe_w1, moe_w3, moe_w2, ln_final):
    assert ln_mix_e.shape[0] == 1 and ln_mix_o.shape[0] == 1, "one even and one odd layer"
    bf = lambda a: a[0].astype(BF16)
    f32 = lambda a: a[0].astype(F32)
    p = {
        "ln_mix_e": f32(ln_mix_e), "w_in_e": bf(w_in_e),
        "dw_conv_w": f32(dw_conv_w), "dw_conv_b": f32(dw_conv_b),
        "conv_ln_g": f32(conv_ln_g), "conv_ln_b": f32(conv_ln_b),
        "rg_conv_w": f32(rg_conv_w), "rg_conv_b": f32(rg_conv_b),
        "rg_w_a": f32(rg_w_a), "rg_b_a": f32(rg_b_a), "rg_w_i": f32(rg_w_i), "rg_b_i": f32(rg_b_i),
        "rg_lam": f32(rg_lam), "w_out_e": bf(w_out_e),
        "ln_ffn_e": f32(ln_ffn_e), "ffn_w1": bf(ffn_w1), "ffn_w3": bf(ffn_w3), "ffn_w2": bf(ffn_w2),
        "ln_mix_o": f32(ln_mix_o), "w_in_o": bf(w_in_o),
        "sgu_ln_g": f32(sgu_ln_g), "sgu_ln_b": f32(sgu_ln_b),
        "sgu_w": bf(sgu_w), "sgu_b": f32(sgu_b), "w_out_o": bf(w_out_o),
        "ln_ffn_o": f32(ln_ffn_o), "w_router": f32(w_router),
        "moe_w1": f32(moe_w1), "moe_w3": f32(moe_w3), "moe_w2": f32(moe_w2),
        "ln_final": ln_final.astype(F32),
    }
    return _trunks([x_prompt, x_sample], p)
```

```python
import functools

import jax
import jax.numpy as jnp
from jax import lax
from jax.experimental import pallas as pl
from jax.experimental.pallas import tpu as pltpu

EPS = 1e-6
LRU_C = 8.0
LANES = 128
SUBLANES = 8
MXU_DIM = 256
VMEM_LIMIT_BYTES = 56 * 1024 * 1024
F32 = jnp.float32
BF16 = jnp.bfloat16


def _params(*sem):
    return pltpu.CompilerParams(dimension_semantics=sem, vmem_limit_bytes=VMEM_LIMIT_BYTES)


def _sigmoid(x):
    return 0.5 * (jnp.tanh(0.5 * x) + 1.0)


def _gelu_tanh(x):
    c = 0.7978845608028654
    hx = 0.5 * x
    return hx + hx * jnp.tanh(x * (c + (c * 0.044715) * (x * x)))


def _rmsnorm_f32(x, g):
    return x * lax.rsqrt(jnp.mean(x * x, axis=-1, keepdims=True) + EPS) * g


def _pick_tile(n, want):
    t = min(n, want)
    while n % t:
        t //= 2
    return t


def _perm_matrix(batch):
    tl = MXU_DIM // batch
    r = jnp.arange(MXU_DIM)
    src = (r % batch) * tl + r // batch
    return (src[:, None] == r[None, :]).astype(BF16)


def _inproj_e_kernel(x_ref, g_ref, p_ref, w_ref, o_ref, xt_ref, *, n_chunk):
    B, tt, _ = x_ref.shape
    tl = MXU_DIM // B
    for q in range(tt // tl):
        xb = jnp.concatenate([x_ref[b, q * tl:(q + 1) * tl, :] for b in range(B)], axis=0)
        xn = _rmsnorm_f32(xb, g_ref[...]).astype(BF16)
        xt_ref[q * MXU_DIM:(q + 1) * MXU_DIM, :] = jnp.dot(
            p_ref[...], xn, preferred_element_type=F32).astype(BF16)
    xt = xt_ref[...]
    n = o_ref.shape[-1]
    for j in range(n // n_chunk):
        cols = slice(j * n_chunk, (j + 1) * n_chunk)
        o_ref[:, cols] = jnp.dot(xt, w_ref[:, cols], preferred_element_type=F32).astype(o_ref.dtype)


def _inproj_e(x, g, w):
    B, S, D = x.shape
    N = w.shape[1]
    assert MXU_DIM % B == 0
    tt = _pick_tile(S, 128)
    assert tt % (MXU_DIM // B) == 0
    return pl.pallas_call(
        functools.partial(_inproj_e_kernel, n_chunk=1024),
        grid=(S // tt,),
        in_specs=[
            pl.BlockSpec((B, tt, D), lambda i: (0, i, 0)),
            pl.BlockSpec((1, D), lambda i: (0, 0)),
            pl.BlockSpec((MXU_DIM, MXU_DIM), lambda i: (0, 0)),
            pl.BlockSpec((D, N), lambda i: (0, 0)),
        ],
        out_specs=pl.BlockSpec((tt * B, N), lambda i: (i, 0)),
        out_shape=jax.ShapeDtypeStruct((S * B, N), BF16),
        scratch_shapes=[pltpu.VMEM((tt * B, D), BF16)],
        compiler_params=_params("parallel"),
        name="inproj_even",
    )(x, g.reshape(1, D), _perm_matrix(B), w)


def _conv_kernel(zp_ref, zm_ref, zn_ref, w_ref, b_ref, g_ref, be_ref, o_ref, s_ref, y_ref,
                 *, rows, halo, cdim, ktaps, batch, unroll, ln_rows, n_steps):
    i = pl.program_id(0)
    n = n_steps

    def glu(z):
        return z[:, :cdim].astype(F32) * _sigmoid(z[:, cdim:].astype(F32))

    s_ref[halo:halo + rows, :] = glu(zm_ref[...])
    s_ref[0:halo, :] = jnp.where(i > 0, glu(zp_ref[...]), 0.0)
    s_ref[halo + rows:, :] = jnp.where(i < n - 1, glu(zn_ref[...]), 0.0)

    pad = ktaps // 2
    base = halo - pad * batch
    step = batch * unroll
    for j in range(cdim // LANES):
        lanes = slice(j * LANES, (j + 1) * LANES)
        wj = [jnp.broadcast_to(w_ref[k:k + 1, lanes], (batch, LANES)) for k in range(ktaps)]
        bj = jnp.broadcast_to(b_ref[:, lanes], (batch, LANES))

        def body(ci, carry, lanes=lanes, wj=wj, bj=bj):
            r0 = pl.multiple_of(ci * step, step)
            accs = [bj] * unroll
            for m in range(unroll + ktaps - 1):
                xm = s_ref[pl.ds(r0 + base + batch * m, batch), lanes]
                for u in range(unroll):
                    k = m - u
                    if 0 <= k < ktaps:
                        accs[u] = accs[u] + wj[k] * xm
            for u in range(unroll):
                y_ref[pl.ds(r0 + batch * u, batch), lanes] = accs[u]
            return carry

        lax.fori_loop(0, rows // step, body, 0)

    def ln_body(ci, carry):
        r0 = pl.multiple_of(ci * ln_rows, ln_rows)
        y = y_ref[pl.ds(r0, ln_rows), :]
        mu = jnp.mean(y, axis=-1, keepdims=True)
        d = y - mu
        var = jnp.mean(d * d, axis=-1, keepdims=True)
        yn = d * lax.rsqrt(var + EPS) * g_ref[...] + be_ref[...]
        o_ref[pl.ds(r0, ln_rows), :] = (yn * _sigmoid(yn)).astype(o_ref.dtype)
        return carry

    lax.fori_loop(0, rows // ln_rows, ln_body, 0)


def _conv_branch(z, w, b, ln_g, ln_b, batch):
    R = z.shape[0]
    ktaps, cdim = w.shape
    halo = 128
    assert (ktaps // 2) * batch <= halo and batch == SUBLANES
    rows = _pick_tile(R, 2048)
    assert rows % halo == 0
    hb = rows // halo
    nh = R // halo
    vec = lambda a: a.reshape(1, cdim)
    return pl.pallas_call(
        functools.partial(_conv_kernel, rows=rows, halo=halo, cdim=cdim, ktaps=ktaps, batch=batch,
                          unroll=8, ln_rows=min(rows, 256), n_steps=R // rows),
        grid=(R // rows,),
        in_specs=[
            pl.BlockSpec((halo, 2 * cdim), lambda i: (jnp.maximum(i * hb - 1, 0), 0)),
            pl.BlockSpec((rows, 2 * cdim), lambda i: (i, 0)),
            pl.BlockSpec((halo, 2 * cdim), lambda i: (jnp.minimum((i + 1) * hb, nh - 1), 0)),
            pl.BlockSpec((ktaps, cdim), lambda i: (0, 0)),
            pl.BlockSpec((1, cdim), lambda i: (0, 0)),
            pl.BlockSpec((1, cdim), lambda i: (0, 0)),
            pl.BlockSpec((1, cdim), lambda i: (0, 0)),
        ],
        out_specs=pl.BlockSpec((rows, cdim), lambda i: (i, 0)),
        out_shape=jax.ShapeDtypeStruct((R, cdim), BF16),
        scratch_shapes=[pltpu.VMEM((rows + 2 * halo, cdim), F32), pltpu.VMEM((rows, cdim), F32)],
        compiler_params=_params("parallel"),
        name="conv_branch",
    )(z, z, z, w, vec(b), vec(ln_g), vec(ln_b))


def _lru_kernel(*refs, rows, batch, ktaps, sub, reverse, final):
    if final:
        (zx_ref, zy_ref, hb_ref, cw_ref, cb_ref, wg_ref, ba_ref, bi_ref, lam_ref,
         o_ref, xs_ref, a_ref, u_ref, h_ref) = refs
    else:
        (zx_ref, cw_ref, cb_ref, wg_ref, ba_ref, bi_ref, lam_ref,
         o_ref, xs_ref, a_ref, u_ref, h_ref) = refs
    hl = (ktaps - 1) * batch
    main = 0 if reverse else hl
    ldim = zx_ref.shape[-1]
    ngrp = wg_ref.shape[0]
    gw = ldim // ngrp

    @pl.when(pl.program_id(0) == 0)
    def _():
        h_ref[...] = jnp.zeros_like(h_ref)
        xs_ref[...] = jnp.zeros_like(xs_ref)

    xs_ref[main:main + rows, :] = zx_ref[...].astype(F32)

    lam = lam_ref[...]
    sp = jnp.maximum(-lam, 0.0) + jnp.log(1.0 + jnp.exp(-jnp.abs(lam)))
    half_neg_c_sp = -0.5 * LRU_C * sp

    def gate_body(ci, carry):
        r0 = pl.multiple_of(ci * sub, sub)
        xc = jnp.broadcast_to(cb_ref[...], (sub, ldim))
        for k in range(ktaps):
            xc = xc + cw_ref[k:k + 1, :] * xs_ref[pl.ds(r0 + batch * k, sub), :]
        xcb = xc.astype(BF16)
        for g in range(ngrp):
            cols = slice(g * gw, (g + 1) * gw)
            gates = jnp.dot(xcb[:, cols], wg_ref[g], preferred_element_type=F32)
            tr = jnp.tanh(gates[:, :gw] + ba_ref[:, cols])
            ig2 = jnp.tanh(gates[:, gw:] + bi_ref[:, cols]) + 1.0
            log_a = (tr + 1.0) * half_neg_c_sp[:, cols]
            a_ref[pl.ds(r0, sub), cols] = jnp.exp(log_a)
            th = jnp.tanh(log_a)
            half_mult = jnp.sqrt(jnp.maximum(0.5 * th / (th - 1.0), 0.25e-12))
            u_ref[pl.ds(r0, sub), cols] = half_mult * (ig2 * xc[:, cols])
        return carry

    lax.fori_loop(0, rows // sub, gate_body, 0)

    if reverse:
        xs_ref[rows:rows + hl, :] = xs_ref[0:hl, :]
    else:
        xs_ref[0:hl, :] = xs_ref[rows:rows + hl, :]

    nt = rows // batch

    def scan_body(t, h):
        tt = (nt - 1 - t) if reverse else t
        r = pl.multiple_of(tt * batch, batch)
        h = a_ref[pl.ds(r, batch), :] * h + u_ref[pl.ds(r, batch), :]
        u_ref[pl.ds(r, batch), :] = h
        return h

    h_ref[...] = lax.fori_loop(0, nt, scan_body, h_ref[...], unroll=8)

    def out_body(ci, carry):
        r0 = pl.multiple_of(ci * sub, sub)
        h = u_ref[pl.ds(r0, sub), :]
        if final:
            h = h + hb_ref[pl.ds(r0, sub), :].astype(F32)
            h = _gelu_tanh(zy_ref[pl.ds(r0, sub), :].astype(F32)) * h
        o_ref[pl.ds(r0, sub), :] = h.astype(o_ref.dtype)
        return carry

    lax.fori_loop(0, rows // sub, out_body, 0)


def _pack_blockdiag(w, width):
    H, dh, _ = w.shape
    per = width // dh
    G = H // per
    w = w.reshape(G, per, dh, dh)
    eye = jnp.eye(per, dtype=w.dtype)
    return jnp.einsum("pq,gpij->gpiqj", eye, w).reshape(G, per * dh, per * dh)


def _lru_pass(z, hb, cw, cb, w_a, b_a, w_i, b_i, lam, *, batch, reverse, col_x, col_y):
    R = z.shape[0]
    ktaps, ldim = cw.shape
    final = hb is not None
    rows = _pick_tile(R, 1024)
    nblk = R // rows
    wg = (0.5 * jnp.concatenate([_pack_blockdiag(w_a, MXU_DIM), _pack_blockdiag(w_i, MXU_DIM)], axis=-1)).astype(BF16)
    b_a, b_i = 0.5 * b_a, 0.5 * b_i
    ngrp = wg.shape[0]
    vec = lambda a: a.reshape(1, ldim).astype(F32)
    blk = (lambda i: nblk - 1 - i) if reverse else (lambda i: i)
    const2 = lambda i: (0, 0)
    in_specs = [pl.BlockSpec((rows, ldim), lambda i: (blk(i), col_x))]
    args = [z]
    if final:
        in_specs += [pl.BlockSpec((rows, ldim), lambda i: (blk(i), col_y)),
                     pl.BlockSpec((rows, ldim), lambda i: (blk(i), 0))]
        args += [z, hb]
    in_specs += [
        pl.BlockSpec((ktaps, ldim), const2),
        pl.BlockSpec((1, ldim), const2),
        pl.BlockSpec((ngrp, MXU_DIM, 2 * MXU_DIM), lambda i: (0, 0, 0)),
        pl.BlockSpec((1, ldim), const2),
        pl.BlockSpec((1, ldim), const2),
        pl.BlockSpec((1, ldim), const2),
    ]
    args += [cw, vec(cb), wg, vec(b_a), vec(b_i), vec(lam)]
    hl = (ktaps - 1) * batch
    return pl.pallas_call(
        functools.partial(_lru_kernel, rows=rows, batch=batch, ktaps=ktaps, sub=min(rows, 256),
                          reverse=reverse, final=final),
        grid=(nblk,),
        in_specs=in_specs,
        out_specs=pl.BlockSpec((rows, ldim), lambda i: (blk(i), 0)),
        out_shape=jax.ShapeDtypeStruct((R, ldim), BF16),
        scratch_shapes=[
            pltpu.VMEM((rows + hl, ldim), F32),
            pltpu.VMEM((rows, ldim), F32),
            pltpu.VMEM((rows, ldim), F32),
            pltpu.VMEM((batch, ldim), F32),
        ],
        compiler_params=_params("arbitrary"),
        name="rglru_bwd" if reverse else "rglru_fwd",
    )(*args)


def _outproj_e_kernel(c_ref, r_ref, x_ref, pt_ref, w_ref, o_ref, m_ref):
    B, tt, _ = x_ref.shape
    tl = MXU_DIM // B
    cdim = c_ref.shape[1]
    for q in range(tt // tl):
        rows = slice(q * MXU_DIM, (q + 1) * MXU_DIM)
        m_ref[rows, :cdim] = jnp.dot(pt_ref[...], c_ref[rows, :], preferred_element_type=F32).astype(BF16)
        m_ref[rows, cdim:] = jnp.dot(pt_ref[...], r_ref[rows, :], preferred_element_type=F32).astype(BF16)
    acc = jnp.dot(m_ref[...], w_ref[...], preferred_element_type=F32)
    for q in range(tt // tl):
        for b in range(B):
            r0 = q * MXU_DIM + b * tl
            ts = slice(q * tl, (q + 1) * tl)
            o_ref[b, ts, :] = x_ref[b, ts, :] + acc[r0:r0 + tl, :]


def _outproj_e(c, r, x, w):
    B, S, D = x.shape
    cdim, ldim = c.shape[1], r.shape[1]
    tt = _pick_tile(S, 128)
    return pl.pallas_call(
        _outproj_e_kernel,
        grid=(S // tt,),
        in_specs=[
            pl.BlockSpec((tt * B, cdim), lambda i: (i, 0)),
            pl.BlockSpec((tt * B, ldim), lambda i: (i, 0)),
            pl.BlockSpec((B, tt, D), lambda i: (0, i, 0)),
            pl.BlockSpec((MXU_DIM, MXU_DIM), lambda i: (0, 0)),
            pl.BlockSpec((cdim + ldim, D), lambda i: (0, 0)),
        ],
        out_specs=pl.BlockSpec((B, tt, D), lambda i: (0, i, 0)),
        out_shape=jax.ShapeDtypeStruct((B, S, D), F32),
        scratch_shapes=[pltpu.VMEM((tt * B, cdim + ldim), BF16)],
        compiler_params=_params("parallel"),
        name="outproj_even",
    )(c, r, x, _perm_matrix(B).T, w)


FFN_COL_CHUNK = 2 * MXU_DIM


def _ffn_kernel(*refs, bounds):
    n_src = len(bounds) - 1
    x_refs = refs[:n_src]
    g_ref, w1_ref, w3_ref, w2_ref, o_ref, xn_ref = refs[n_src:]
    i = pl.program_id(0)
    for x_ref, lo, hi in zip(x_refs, bounds[:-1], bounds[1:]):
        @pl.when((i >= lo) & (i < hi))
        def _(x_ref=x_ref):
            x = x_ref[...]
            xn_ref[...] = _rmsnorm_f32(x, g_ref[...]).astype(BF16)
            o_ref[...] = x

    xn = xn_ref[...]
    F = w1_ref.shape[1]
    acc = None
    for c0 in range(0, F, FFN_COL_CHUNK):
        c1 = min(c0 + FFN_COL_CHUNK, F)
        a = jnp.dot(xn, w1_ref[:, c0:c1], preferred_element_type=F32)
        b = jnp.dot(xn, w3_ref[:, c0:c1], preferred_element_type=F32)
        act = (a * _sigmoid(a) * b).astype(BF16)
        part = jnp.dot(act, w2_ref[c0:c1, :], preferred_element_type=F32)
        acc = part if acc is None else acc + part
    o_ref[...] += acc


def _ffn(xs, g, w1, w3, w2, *, tm=1024):
    D = xs[0].shape[1]
    F = w1.shape[1]
    assert all(x.shape[0] % tm == 0 for x in xs)
    counts = [x.shape[0] // tm for x in xs]
    starts = [sum(counts[:k]) for k in range(len(xs))]
    T = tm * sum(counts)

    def src_spec(lo, n):
        return pl.BlockSpec((tm, D), lambda i: (jnp.clip(i - lo, 0, n - 1), 0))

    resident = dict(pipeline_mode=pl.Buffered(1))
    return pl.pallas_call(
        functools.partial(_ffn_kernel, bounds=tuple(starts) + (T // tm,)),
        grid=(T // tm,),
        in_specs=[src_spec(lo, n) for lo, n in zip(starts, counts)] + [
            pl.BlockSpec((1, D), lambda i: (0, 0)),
            pl.BlockSpec((D, F), lambda i: (0, 0), **resident),
            pl.BlockSpec((D, F), lambda i: (0, 0), **resident),
            pl.BlockSpec((F, D), lambda i: (0, 0), **resident),
        ],
        out_specs=pl.BlockSpec((tm, D), lambda i: (i, 0)),
        out_shape=jax.ShapeDtypeStruct((T, D), F32),
        scratch_shapes=[pltpu.VMEM((tm, D), BF16)],
        compiler_params=_params("parallel"),
        name="ffn_swiglu",
    )(*xs, g.reshape(1, D), w1, w3, w2)


def _inproj_o_kernel(x_ref, g_ref, w_ref, lg_ref, lb_ref, u_ref, v_ref, zv_ref, *, sdim, n_chunk):
    xn = _rmsnorm_f32(x_ref[...], g_ref[...]).astype(BF16)
    for j in range(sdim // n_chunk):
        cols = slice(j * n_chunk, (j + 1) * n_chunk)
        zu = _gelu_tanh(jnp.dot(xn, w_ref[:, cols], preferred_element_type=F32))
        u_ref[:, cols] = zu.astype(u_ref.dtype)
    for j in range(sdim // n_chunk):
        cols = slice(j * n_chunk, (j + 1) * n_chunk)
        wcols = slice(sdim + j * n_chunk, sdim + (j + 1) * n_chunk)
        zv_ref[:, cols] = _gelu_tanh(jnp.dot(xn, w_ref[:, wcols], preferred_element_type=F32))
    zv = zv_ref[...]
    mu = jnp.mean(zv, axis=-1, keepdims=True)
    d = zv - mu
    var = jnp.mean(d * d, axis=-1, keepdims=True)
    v_ref[...] = (d * lax.rsqrt(var + EPS) * lg_ref[...] + lb_ref[...]).astype(v_ref.dtype)


def _inproj_o(x, g, w, ln_g, ln_b):
    T, D = x.shape
    sdim = w.shape[1] // 2
    tm = _pick_tile(T, 512)
    row = lambda i: (i, 0)
    const = lambda i: (0, 0)
    return pl.pallas_call(
        functools.partial(_inproj_o_kernel, sdim=sdim, n_chunk=_pick_tile(sdim, 512)),
        grid=(T // tm,),
        scratch_shapes=[pltpu.VMEM((tm, sdim), F32)],
        in_specs=[
            pl.BlockSpec((tm, D), row),
            pl.BlockSpec((1, D), const),
            pl.BlockSpec((D, 2 * sdim), const),
            pl.BlockSpec((1, sdim), const),
            pl.BlockSpec((1, sdim), const),
        ],
        out_specs=[pl.BlockSpec((tm, sdim), row), pl.BlockSpec((tm, sdim), row)],
        out_shape=[jax.ShapeDtypeStruct((T, sdim), BF16), jax.ShapeDtypeStruct((T, sdim), BF16)],
        compiler_params=_params("parallel"),
        name="inproj_odd",
    )(x, g.reshape(1, D), w, ln_g.reshape(1, sdim), ln_b.reshape(1, sdim))


def _sgu_out_kernel(u_ref, v_ref, sw_ref, sb_ref, w_ref, x_ref, o_ref, g_ref, *, chunk, heads):
    tm, sdim = u_ref.shape
    hd = sdim // heads
    for n in range(tm // chunk):
        rows = slice(n * chunk, (n + 1) * chunk)
        for h in range(heads):
            lanes = slice(h * hd, (h + 1) * hd)
            sv = jnp.dot(sw_ref[h], v_ref[rows, lanes], preferred_element_type=F32) + sb_ref[h]
            g_ref[rows, lanes] = (u_ref[rows, lanes].astype(F32) * sv).astype(BF16)
    o_ref[...] = x_ref[...] + jnp.dot(g_ref[...], w_ref[...], preferred_element_type=F32)


def _sgu_out(u, v, sgu_w, sgu_b, w, x):
    T, sdim = u.shape
    D = x.shape[1]
    heads, chunk, _ = sgu_w.shape
    hd = sdim // heads
    tm = _pick_tile(T, 512)
    assert tm % chunk == 0
    sb = jnp.broadcast_to(sgu_b.astype(F32)[:, :, None], (heads, chunk, hd))
    row = lambda i: (i, 0)
    return pl.pallas_call(
        functools.partial(_sgu_out_kernel, chunk=chunk, heads=heads),
        grid=(T // tm,),
        in_specs=[
            pl.BlockSpec((tm, sdim), row),
            pl.BlockSpec((tm, sdim), row),
            pl.BlockSpec((heads, chunk, chunk), lambda i: (0, 0, 0)),
            pl.BlockSpec((heads, chunk, hd), lambda i: (0, 0, 0)),
            pl.BlockSpec((sdim, D), lambda i: (0, 0)),
            pl.BlockSpec((tm, D), row),
        ],
        out_specs=pl.BlockSpec((tm, D), row),
        out_shape=jax.ShapeDtypeStruct((T, D), F32),
        scratch_shapes=[pltpu.VMEM((tm, sdim), BF16)],
        compiler_params=_params("parallel"),
        name="sgu_outproj_odd",
    )(u, v, sgu_w, sb, w, x)


META_E1, META_E2, META_R1, META_R2, META_G1, META_G2 = range(6)
ROUTE_TILE = 512


def _router_kernel(x_ref, g_ref, wr_ref, meta_ref, route_ref, cnt_ref, run_ref, *, n_experts):
    @pl.when(pl.program_id(0) == 0)
    def _():
        run_ref[...] = jnp.zeros_like(run_ref)

    xn = _rmsnorm_f32(x_ref[...], g_ref[...])
    xh = xn.astype(BF16)
    xl = (xn - xh.astype(F32)).astype(BF16)
    logits = (jnp.dot(xh, wr_ref[0], preferred_element_type=F32)
              + jnp.dot(xl, wr_ref[0], preferred_element_type=F32)
              + jnp.dot(xh, wr_ref[1], preferred_element_type=F32))
    tm = logits.shape[0]
    lane = lax.broadcasted_iota(jnp.int32, logits.shape, 1).astype(F32)
    neg = jnp.float32(-jnp.inf)
    big = jnp.float32(1e9)
    l1 = jnp.where(lane < n_experts, logits, neg)
    m1 = jnp.max(l1, axis=-1, keepdims=True)
    i1 = jnp.min(jnp.where(l1 == m1, lane, big), axis=-1, keepdims=True)
    l2 = jnp.where(lane == i1, neg, l1)
    m2 = jnp.max(l2, axis=-1, keepdims=True)
    i2 = jnp.min(jnp.where(l2 == m2, lane, big), axis=-1, keepdims=True)
    e2 = jnp.exp(m2 - m1)
    den = 1.0 + e2
    g1 = 1.0 / den
    g2 = e2 / den

    oh1 = jnp.where(lane == i1, 1.0, 0.0)
    oh2 = jnp.where(lane == i2, 1.0, 0.0)
    cnt = oh1 + oh2
    ri = lax.broadcasted_iota(jnp.int32, (tm, tm), 0)
    ci = lax.broadcasted_iota(jnp.int32, (tm, tm), 1)
    lower = jnp.where(ci < ri, 1.0, 0.0).astype(BF16)
    prefix = jnp.dot(lower, cnt.astype(BF16), preferred_element_type=F32) + run_ref[...]
    r1 = jnp.sum(oh1 * prefix, axis=-1, keepdims=True)
    r2 = jnp.sum(oh2 * prefix, axis=-1, keepdims=True)
    run_ref[...] += jnp.sum(cnt, axis=0, keepdims=True)

    meta = jnp.zeros_like(logits)
    for idx, val in ((META_E1, i1), (META_E2, i2), (META_R1, r1), (META_R2, r2), (META_G1, g1), (META_G2, g2)):
        meta = jnp.where(lane == idx, val, meta)
    meta_ref[...] = meta
    route_ref[...] = meta.T[:route_ref.shape[0], :]
    cnt_ref[...] = jnp.broadcast_to(run_ref[...], cnt_ref.shape)


def _router(x, g, w_router):
    T, D = x.shape
    E = w_router.shape[1]
    wr = jnp.zeros((D, LANES), F32).at[:, :E].set(w_router.astype(F32))
    wr_hi = wr.astype(BF16)
    wr = jnp.stack([wr_hi, (wr - wr_hi.astype(F32)).astype(BF16)])
    tm = ROUTE_TILE
    assert T % tm == 0
    row = lambda i: (i, 0)
    return pl.pallas_call(
        functools.partial(_router_kernel, n_experts=E),
        grid=(T // tm,),
        in_specs=[
            pl.BlockSpec((tm, D), row),
            pl.BlockSpec((1, D), lambda i: (0, 0)),
            pl.BlockSpec((2, D, LANES), lambda i: (0, 0, 0)),
        ],
        out_specs=[pl.BlockSpec((tm, LANES), row), pl.BlockSpec((SUBLANES, tm), lambda i: (0, i)),
                   pl.BlockSpec((SUBLANES, LANES), lambda i: (0, 0))],
        out_shape=[jax.ShapeDtypeStruct((T, LANES), F32), jax.ShapeDtypeStruct((SUBLANES, T), F32),
                   jax.ShapeDtypeStruct((SUBLANES, LANES), F32)],
        scratch_shapes=[pltpu.VMEM((1, LANES), F32)],
        compiler_params=_params("arbitrary"),
        name="moe_router",
    )(x, g.reshape(1, D), wr)


def _route_plan(route, cnt, n_experts, tmx, n_tiles):
    counts = cnt[0, :n_experts].astype(jnp.int32)
    nt = (counts + tmx - 1) // tmx
    tend = jnp.cumsum(nt)
    off = (tend - nt) * tmx
    eids = jnp.arange(n_experts, dtype=jnp.int32)

    def pos(e_row, r_row):
        e = route[e_row].astype(jnp.int32)
        base = jnp.sum(jnp.where(e[None, :] == eids[:, None], off[:, None], 0), axis=0)
        return base + route[r_row].astype(jnp.int32)

    T = route.shape[1]
    tm = ROUTE_TILE
    p1 = pos(META_E1, META_R1).reshape(T // tm, 1, tm)
    p2 = pos(META_E2, META_R2).reshape(T // tm, 1, tm)
    tiles = jnp.arange(n_tiles, dtype=jnp.int32)
    tile_expert = jnp.minimum(jnp.sum((tend[None, :] <= tiles[:, None]).astype(jnp.int32), axis=1), n_experts - 1)
    plan = dict(pos=jnp.concatenate([p1, p2], axis=2), tile_expert=tile_expert.astype(jnp.int32),
                n_active=tend[-1:].astype(jnp.int32), pad_start=(off + counts).astype(jnp.int32),
                pad_len=(nt * tmx - counts).astype(jnp.int32))
    return plan


def _dispatch_kernel(pstart_ref, plen_ref, na_ref, pos_ref, x_ref, g_ref, w1_ref, w3_ref, w2_ref,
                     xs_ref, w1o_ref, w3o_ref, w2o_ref, xn_ref, z_ref, sem, zsem,
                     *, pad_bits, n_steps, tmx):
    i = pl.program_id(0)
    tm = x_ref.shape[0]
    slot = i % 2

    third = n_steps // 3
    for k, (src_ref, dst_ref) in enumerate(((w1_ref, w1o_ref), (w3_ref, w3o_ref), (w2_ref, w2o_ref))):
        @pl.when((i >= k * third) & (i < (k + 1) * third))
        def _(src_ref=src_ref, dst_ref=dst_ref):
            dst_ref[...] = src_ref[...].astype(dst_ref.dtype)

    @pl.when(i == 0)
    def _():
        z_ref[...] = jnp.zeros_like(z_ref)

        def zero_rows(dst, rows):
            cp = pltpu.make_async_copy(z_ref.at[pl.ds(0, rows)], xs_ref.at[pl.ds(dst, rows)], zsem)
            cp.start()
            cp.wait()

        for e in range(pstart_ref.shape[0]):
            n = plen_ref[e]
            s = pstart_ref[e]
            head = n & (SUBLANES - 1)
            for j in range(SUBLANES - 1):
                @pl.when(j < head)
                def _(j=j, s=s):
                    zero_rows(s + j, 1)

            groups = n >> 3
            base = s + head
            for k in reversed(range(pad_bits - 3)):
                @pl.when(((groups >> k) & 1) == 1)
                def _(k=k, groups=groups, base=base):
                    dst = pl.multiple_of(base + (((groups >> (k + 1)) << (k + 1)) << 3), SUBLANES)
                    zero_rows(dst, SUBLANES << k)

        zrows = z_ref.shape[0]
        n_tiles = xs_ref.shape[0] // tmx
        for t in range(n_tiles - pstart_ref.shape[0], n_tiles):
            @pl.when(t >= na_ref[0])
            def _(t=t):
                for c in range(tmx // zrows):
                    zero_rows(t * tmx + c * zrows, zrows)

    xn = _rmsnorm_f32(x_ref[...], g_ref[...])

    for s in range(2):
        @pl.when(slot == s)
        def _(s=s):
            xn_ref[s] = xn

            def issue(q, carry):
                r0 = pl.multiple_of(q * SUBLANES, SUBLANES)
                for j in range(SUBLANES):
                    src = xn_ref.at[s, pl.ds(r0 + j, 1)]
                    pltpu.make_async_copy(src, xs_ref.at[pl.ds(pos_ref[0, r0 + j], 1)], sem.at[s]).start()
                    pltpu.make_async_copy(src, xs_ref.at[pl.ds(pos_ref[0, tm + r0 + j], 1)], sem.at[s]).start()
                return carry

            lax.fori_loop(0, tm // SUBLANES, issue, 0)

    def drain(s):
        for _ in range(2):
            pltpu.make_async_copy(xn_ref.at[s], xs_ref.at[pl.ds(0, tm)], sem.at[s]).wait()

    @pl.when(i > 0)
    def _():
        drain(1 - slot)

    @pl.when(i == n_steps - 1)
    def _():
        drain(slot)


def _dispatch(x, g, plan, n_rows, tmx, w1, w3, w2):
    T, D = x.shape
    tm = ROUTE_TILE
    n = T // tm
    pad_bits = (tmx - 1).bit_length()
    assert n_rows % tmx == 0 and tmx % (1 << (pad_bits - 1)) == 0
    E, _, F = w1.shape
    third = n // 3
    assert n % 3 == 0 and (E * D) % third == 0 and (E * F) % third == 0
    r13, r2 = (E * D) // third, (E * F) // third
    assert r13 % (2 * SUBLANES) == 0 and r2 % (2 * SUBLANES) == 0

    def slab(rows, width, k):
        return pl.BlockSpec((rows, width), lambda i, ps, pn, na: (jnp.clip(i - k * third, 0, third - 1), 0))

    grid_spec = pltpu.PrefetchScalarGridSpec(
        num_scalar_prefetch=3,
        grid=(n,),
        in_specs=[
            pl.BlockSpec((None, 1, 2 * tm), lambda i, ps, pn, na: (i, 0, 0), memory_space=pltpu.SMEM),
            pl.BlockSpec((tm, D), lambda i, ps, pn, na: (i, 0)),
            pl.BlockSpec((1, D), lambda i, ps, pn, na: (0, 0)),
            slab(r13, F, 0), slab(r13, F, 1), slab(r2, D, 2),
        ],
        out_specs=[pl.BlockSpec(memory_space=pl.ANY), slab(r13, F, 0), slab(r13, F, 1), slab(r2, D, 2)],
        scratch_shapes=[
            pltpu.VMEM((2, tm, D), F32),
            pltpu.VMEM((1 << (pad_bits - 1), D), F32),
            pltpu.SemaphoreType.DMA((2,)),
            pltpu.SemaphoreType.DMA(()),
        ],
    )
    xs, w1b, w3b, w2b = pl.pallas_call(
        functools.partial(_dispatch_kernel, pad_bits=pad_bits, n_steps=n, tmx=tmx),
        grid_spec=grid_spec,
        out_shape=[jax.ShapeDtypeStruct((n_rows, D), F32), jax.ShapeDtypeStruct((E * D, F), BF16),
                   jax.ShapeDtypeStruct((E * D, F), BF16), jax.ShapeDtypeStruct((E * F, D), BF16)],
        compiler_params=_params("arbitrary"),
        name="moe_dispatch",
    )(plan["pad_start"], plan["pad_len"], plan["n_active"], plan["pos"], x, g.reshape(1, D),
      w1.reshape(E * D, F), w3.reshape(E * D, F), w2.reshape(E * F, D))
    return xs, w1b.reshape(E, D, F), w3b.reshape(E, D, F), w2b.reshape(E, F, D)


MOE_COL_CHUNK = 4 * MXU_DIM


def _moe_experts_kernel(te_ref, na_ref, x_ref, w1_ref, w3_ref, w2_ref, o_ref, xb_ref):
    del te_ref
    f = pl.program_id(1)
    active = pl.program_id(0) < na_ref[0]

    @pl.when(f == 0)
    def _():
        o_ref[...] = jnp.zeros_like(o_ref)

    @pl.when((f == 0) & active)
    def _():
        xb_ref[...] = x_ref[...].astype(BF16)

    @pl.when(active)
    def _():
        xn = xb_ref[...]
        tf = w1_ref.shape[1]
        acc = None
        for c0 in range(0, tf, MOE_COL_CHUNK):
            c1 = min(c0 + MOE_COL_CHUNK, tf)
            a = jnp.dot(xn, w1_ref[:, c0:c1], preferred_element_type=F32)
            b = jnp.dot(xn, w3_ref[:, c0:c1], preferred_element_type=F32)
            act = (a * _sigmoid(a) * b).astype(BF16)
            part = jnp.dot(act, w2_ref[c0:c1, :], preferred_element_type=F32)
            acc = part if acc is None else acc + part
        o_ref[...] += acc


def _moe_experts(xs, tile_expert, n_active, w1, w3, w2, *, tmx, tf):
    n_rows, D = xs.shape
    E, _, F = w1.shape
    assert F % tf == 0 and n_rows % tmx == 0
    nf = F // tf
    fsel = lambda i, f, na: jnp.where(i < na[0], f, nf - 1)
    grid_spec = pltpu.PrefetchScalarGridSpec(
        num_scalar_prefetch=2,
        grid=(n_rows // tmx, nf),
        in_specs=[
            pl.BlockSpec((tmx, D), lambda i, f, te, na: (jnp.minimum(i, na[0] - 1), 0)),
            pl.BlockSpec((None, D, tf), lambda i, f, te, na: (te[i], 0, fsel(i, f, na))),
            pl.BlockSpec((None, D, tf), lambda i, f, te, na: (te[i], 0, fsel(i, f, na))),
            pl.BlockSpec((None, tf, D), lambda i, f, te, na: (te[i], fsel(i, f, na), 0)),
        ],
        out_specs=pl.BlockSpec((tmx, D), lambda i, f, te, na: (i, 0)),
        scratch_shapes=[pltpu.VMEM((tmx, D), BF16)],
    )
    return pl.pallas_call(
        _moe_experts_kernel,
        grid_spec=grid_spec,
        out_shape=jax.ShapeDtypeStruct((n_rows, D), F32),
        compiler_params=_params("parallel", "arbitrary"),
        name="moe_experts",
    )(tile_expert, n_active, xs, w1, w3, w2)


def _combine_kernel(pos_ref, posn_ref, meta_ref, res_ref, lnf_ref, y_ref, o_ref, ya_ref, yb_ref, sem, *, n_steps):
    i = pl.program_id(0)
    tm = res_ref.shape[0]
    slot = i % 2

    def gather(p_ref, s):
        def issue(q, carry):
            r0 = pl.multiple_of(q * SUBLANES, SUBLANES)
            for j in range(SUBLANES):
                r = r0 + j
                pltpu.make_async_copy(y_ref.at[pl.ds(p_ref[0, r], 1)], ya_ref.at[s, pl.ds(r, 1)], sem.at[s]).start()
                pltpu.make_async_copy(y_ref.at[pl.ds(p_ref[0, tm + r], 1)], yb_ref.at[s, pl.ds(r, 1)],
                                      sem.at[s]).start()
            return carry

        lax.fori_loop(0, tm // SUBLANES, issue, 0)

    @pl.when(i == 0)
    def _():
        gather(pos_ref, 0)

    for s in range(2):
        @pl.when((i + 1 < n_steps) & (slot == 1 - s))
        def _(s=s):
            gather(posn_ref, s)

    pltpu.make_async_copy(y_ref.at[pl.ds(0, tm)], ya_ref.at[slot], sem.at[slot]).wait()
    pltpu.make_async_copy(y_ref.at[pl.ds(0, tm)], yb_ref.at[slot], sem.at[slot]).wait()

    meta = meta_ref[...]
    g1 = meta[:, META_G1:META_G1 + 1]
    g2 = meta[:, META_G2:META_G2 + 1]
    h = res_ref[...] + g1 * ya_ref[slot] + g2 * yb_ref[slot]
    o_ref[...] = _rmsnorm_f32(h, lnf_ref[...])


def _combine(y, pos, meta, res, ln_final, t_len, t_off):
    D = res.shape[1]
    tm = ROUTE_TILE
    assert t_len % tm == 0 and t_off % tm == 0
    off = t_off // tm
    last = off + t_len // tm - 1
    return pl.pallas_call(
        functools.partial(_combine_kernel, n_steps=t_len // tm),
        grid=(t_len // tm,),
        in_specs=[
            pl.BlockSpec((None, 1, 2 * tm), lambda i: (off + i, 0, 0), memory_space=pltpu.SMEM),
            pl.BlockSpec((None, 1, 2 * tm), lambda i: (jnp.minimum(off + i + 1, last), 0, 0),
                         memory_space=pltpu.SMEM),
            pl.BlockSpec((tm, LANES), lambda i: (off + i, 0)),
            pl.BlockSpec((tm, D), lambda i: (off + i, 0)),
            pl.BlockSpec((1, D), lambda i: (0, 0)),
            pl.BlockSpec(memory_space=pl.ANY),
        ],
        out_specs=pl.BlockSpec((tm, D), lambda i: (i, 0)),
        out_shape=jax.ShapeDtypeStruct((t_len, D), F32),
        scratch_shapes=[pltpu.VMEM((2, tm, D), F32), pltpu.VMEM((2, tm, D), F32), pltpu.SemaphoreType.DMA((2,))],
        compiler_params=_params("arbitrary"),
        name="moe_combine",
    )(pos, pos, meta, res, ln_final.reshape(1, D), y)


MOE_ROW_TILE = 1024
MOE_FF_TILE = 1792


def _even_mixer(x, p):
    B = x.shape[0]
    cdim = p["dw_conv_w"].shape[-1]
    ldim = p["rg_lam"].shape[-1]
    assert cdim * 2 == ldim and p["w_in_e"].shape[-1] == 2 * cdim + 2 * ldim
    z = _inproj_e(x, p["ln_mix_e"], p["w_in_e"])
    c = _conv_branch(z, p["dw_conv_w"], p["dw_conv_b"], p["conv_ln_g"], p["conv_ln_b"], B)
    lru = functools.partial(_lru_pass, z, batch=B, col_x=2, col_y=1)
    hb = lru(None, p["rg_conv_w"][1], p["rg_conv_b"][1], p["rg_w_a"][1], p["rg_b_a"][1],
             p["rg_w_i"][1], p["rg_b_i"][1], p["rg_lam"][1], reverse=True)
    r = lru(hb, p["rg_conv_w"][0], p["rg_conv_b"][0], p["rg_w_a"][0], p["rg_b_a"][0],
            p["rg_w_i"][0], p["rg_b_i"][0], p["rg_lam"][0], reverse=False)
    return _outproj_e(c, r, x, p["w_out_e"])


def _trunks(xs, p):
    D = xs[0].shape[-1]
    lens = [x.shape[0] * x.shape[1] for x in xs]
    offs = [sum(lens[:k]) for k in range(len(xs))]
    T = sum(lens)

    h = _ffn([_even_mixer(x, p).reshape(n, D) for x, n in zip(xs, lens)],
             p["ln_ffn_e"], p["ffn_w1"], p["ffn_w3"], p["ffn_w2"])

    u, v = _inproj_o(h, p["ln_mix_o"], p["w_in_o"], p["sgu_ln_g"], p["sgu_ln_b"])
    h = _sgu_out(u, v, p["sgu_w"], p["sgu_b"], p["w_out_o"], h)

    E = p["w_router"].shape[1]
    tmx = MOE_ROW_TILE
    tf = MOE_FF_TILE if p["moe_w1"].shape[-1] % MOE_FF_TILE == 0 else _pick_tile(p["moe_w1"].shape[-1], 512)
    n_tiles = (2 * T) // tmx + E
    meta, route, cnt = _router(h, p["ln_ffn_o"], p["w_router"])
    plan = _route_plan(route, cnt, E, tmx, n_tiles)
    xsort, w1, w3, w2 = _dispatch(h, p["ln_ffn_o"], plan, n_tiles * tmx, tmx,
                                  p["moe_w1"], p["moe_w3"], p["moe_w2"])
    y = _moe_experts(xsort, plan["tile_expert"], plan["n_active"], w1, w3, w2, tmx=tmx, tf=tf)
    return tuple(_combine(y, plan["pos"], meta, h, p["ln_final"], n, off).reshape(x.shape)
                 for x, n, off in zip(xs, lens, offs))


def kernel(x_prompt, x_sample, ln_mix_e, w_in_e, dw_conv_w, dw_conv_b, conv_ln_g, conv_ln_b, rg_conv_w, rg_conv_b, rg_w_a, rg_b_a, rg_w_i, rg_b_i, rg_lam, w_out_e, ln_ffn_e, ffn_w1, ffn_w3, ffn_w2, ln_mix_o, w_in_o, sgu_ln_g, sgu_ln_b, sgu_w, sgu_b, w_out_o, ln_ffn_o, w_router, moe_w1, moe_w3, moe_w2, ln_final):
    assert ln_mix_e.shape[0] == 1 and ln_mix_o.shape[0] == 1, "one even and one odd layer"
    bf = lambda a: a[0].astype(BF16)
    f32 = lambda a: a[0].astype(F32)
    p = {
        "ln_mix_e": f32(ln_mix_e), "w_in_e": bf(w_in_e),
        "dw_conv_w": f32(dw_conv_w), "dw_conv_b": f32(dw_conv_b),
        "conv_ln_g": f32(conv_ln_g), "conv_ln_b": f32(conv_ln_b),
        "rg_conv_w": f32(rg_conv_w), "rg_conv_b": f32(rg_conv_b),
        "rg_w_a": f32(rg_w_a), "rg_b_a": f32(rg_b_a), "rg_w_i": f32(rg_w_i), "rg_b_i": f32(rg_b_i),
        "rg_lam": f32(rg_lam), "w_out_e": bf(w_out_e),
        "ln_ffn_e": f32(ln_ffn_e), "ffn_w1": bf(ffn_w1), "ffn_w3": bf(ffn_w3), "ffn_w2": bf(ffn_w2),
        "ln_mix_o": f32(ln_mix_o), "w_in_o": bf(w_in_o),
        "sgu_ln_g": f32(sgu_ln_g), "sgu_ln_b": f32(sgu_ln_b),
        "sgu_w": bf(sgu_w), "sgu_b": f32(sgu_b), "w_out_o": bf(w_out_o),
        "ln_ffn_o": f32(ln_ffn_o), "w_router": f32(w_router),
        "moe_w1": f32(moe_w1), "moe_w3": f32(moe_w3), "moe_w2": f32(moe_w2),
        "ln_final": ln_final.astype(F32),
    }
    return _trunks([x_prompt, x_sample], p)
```

```python
import functools

import jax
import jax.numpy as jnp
from jax import lax
from jax.experimental import pallas as pl
from jax.experimental.pallas import tpu as pltpu

EPS = 1e-6
LRU_C = 8.0
LANES = 128
SUBLANES = 8
MXU_DIM = 256
VMEM_LIMIT_BYTES = 56 * 1024 * 1024
F32 = jnp.float32
BF16 = jnp.bfloat16


def _params(*sem):
    return pltpu.CompilerParams(dimension_semantics=sem, vmem_limit_bytes=VMEM_LIMIT_BYTES)


def _sigmoid(x):
    return 0.5 * (jnp.tanh(0.5 * x) + 1.0)


def _gelu_tanh(x):
    c = 0.7978845608028654
    hx = 0.5 * x
    return hx + hx * jnp.tanh(x * (c + (c * 0.044715) * (x * x)))


def _rmsnorm_f32(x, g):
    return x * lax.rsqrt(jnp.mean(x * x, axis=-1, keepdims=True) + EPS) * g


def _pick_tile(n, want):
    t = min(n, want)
    while n % t:
        t //= 2
    return t


def _perm_matrix(batch):
    tl = MXU_DIM // batch
    r = jnp.arange(MXU_DIM)
    src = (r % batch) * tl + r // batch
    return (src[:, None] == r[None, :]).astype(BF16)


def _inproj_e_kernel(x_ref, g_ref, p_ref, w_ref, o_ref, xt_ref, *, n_chunk):
    B, tt, _ = x_ref.shape
    tl = MXU_DIM // B
    for q in range(tt // tl):
        xb = jnp.concatenate([x_ref[b, q * tl:(q + 1) * tl, :] for b in range(B)], axis=0)
        xn = _rmsnorm_f32(xb, g_ref[...]).astype(BF16)
        xt_ref[q * MXU_DIM:(q + 1) * MXU_DIM, :] = jnp.dot(
            p_ref[...], xn, preferred_element_type=F32).astype(BF16)
    xt = xt_ref[...]
    n = o_ref.shape[-1]
    for j in range(n // n_chunk):
        cols = slice(j * n_chunk, (j + 1) * n_chunk)
        o_ref[:, cols] = jnp.dot(xt, w_ref[:, cols], preferred_element_type=F32).astype(o_ref.dtype)


def _inproj_e(x, g, w):
    B, S, D = x.shape
    N = w.shape[1]
    assert MXU_DIM % B == 0
    tt = _pick_tile(S, 128)
    assert tt % (MXU_DIM // B) == 0
    return pl.pallas_call(
        functools.partial(_inproj_e_kernel, n_chunk=1024),
        grid=(S // tt,),
        in_specs=[
            pl.BlockSpec((B, tt, D), lambda i: (0, i, 0)),
            pl.BlockSpec((1, D), lambda i: (0, 0)),
            pl.BlockSpec((MXU_DIM, MXU_DIM), lambda i: (0, 0)),
            pl.BlockSpec((D, N), lambda i: (0, 0)),
        ],
        out_specs=pl.BlockSpec((tt * B, N), lambda i: (i, 0)),
        out_shape=jax.ShapeDtypeStruct((S * B, N), BF16),
        scratch_shapes=[pltpu.VMEM((tt * B, D), BF16)],
        compiler_params=_params("parallel"),
        name="inproj_even",
    )(x, g.reshape(1, D), _perm_matrix(B), w)


def _conv_kernel(zp_ref, zm_ref, zn_ref, w_ref, b_ref, g_ref, be_ref, o_ref, s_ref, y_ref,
                 *, rows, halo, cdim, ktaps, batch, unroll, ln_rows, n_steps):
    i = pl.program_id(0)
    n = n_steps

    def glu(z):
        return z[:, :cdim].astype(F32) * _sigmoid(z[:, cdim:].astype(F32))

    s_ref[halo:halo + rows, :] = glu(zm_ref[...])
    s_ref[0:halo, :] = jnp.where(i > 0, glu(zp_ref[...]), 0.0)
    s_ref[halo + rows:, :] = jnp.where(i < n - 1, glu(zn_ref[...]), 0.0)

    pad = ktaps // 2
    base = halo - pad * batch
    step = batch * unroll
    for j in range(cdim // LANES):
        lanes = slice(j * LANES, (j + 1) * LANES)
        wj = [jnp.broadcast_to(w_ref[k:k + 1, lanes], (batch, LANES)) for k in range(ktaps)]
        bj = jnp.broadcast_to(b_ref[:, lanes], (batch, LANES))

        def body(ci, carry, lanes=lanes, wj=wj, bj=bj):
            r0 = pl.multiple_of(ci * step, step)
            accs = [bj] * unroll
            for m in range(unroll + ktaps - 1):
                xm = s_ref[pl.ds(r0 + base + batch * m, batch), lanes]
                for u in range(unroll):
                    k = m - u
                    if 0 <= k < ktaps:
                        accs[u] = accs[u] + wj[k] * xm
            for u in range(unroll):
                y_ref[pl.ds(r0 + batch * u, batch), lanes] = accs[u]
            return carry

        lax.fori_loop(0, rows // step, body, 0)

    def ln_body(ci, carry):
        r0 = pl.multiple_of(ci * ln_rows, ln_rows)
        y = y_ref[pl.ds(r0, ln_rows), :]
        mu = jnp.mean(y, axis=-1, keepdims=True)
        d = y - mu
        var = jnp.mean(d * d, axis=-1, keepdims=True)
        yn = d * lax.rsqrt(var + EPS) * g_ref[...] + be_ref[...]
        o_ref[pl.ds(r0, ln_rows), :] = (yn * _sigmoid(yn)).astype(o_ref.dtype)
        return carry

    lax.fori_loop(0, rows // ln_rows, ln_body, 0)


def _conv_branch(z, w, b, ln_g, ln_b, batch):
    R = z.shape[0]
    ktaps, cdim = w.shape
    halo = 128
    assert (ktaps // 2) * batch <= halo and batch == SUBLANES
    rows = _pick_tile(R, 2048)
    assert rows % halo == 0
    hb = rows // halo
    nh = R // halo
    vec = lambda a: a.reshape(1, cdim)
    return pl.pallas_call(
        functools.partial(_conv_kernel, rows=rows, halo=halo, cdim=cdim, ktaps=ktaps, batch=batch,
                          unroll=16, ln_rows=min(rows, 1024), n_steps=R // rows),
        grid=(R // rows,),
        in_specs=[
            pl.BlockSpec((halo, 2 * cdim), lambda i: (jnp.maximum(i * hb - 1, 0), 0)),
            pl.BlockSpec((rows, 2 * cdim), lambda i: (i, 0)),
            pl.BlockSpec((halo, 2 * cdim), lambda i: (jnp.minimum((i + 1) * hb, nh - 1), 0)),
            pl.BlockSpec((ktaps, cdim), lambda i: (0, 0)),
            pl.BlockSpec((1, cdim), lambda i: (0, 0)),
            pl.BlockSpec((1, cdim), lambda i: (0, 0)),
            pl.BlockSpec((1, cdim), lambda i: (0, 0)),
        ],
        out_specs=pl.BlockSpec((rows, cdim), lambda i: (i, 0)),
        out_shape=jax.ShapeDtypeStruct((R, cdim), BF16),
        scratch_shapes=[pltpu.VMEM((rows + 2 * halo, cdim), F32), pltpu.VMEM((rows, cdim), F32)],
        compiler_params=_params("parallel"),
        name="conv_branch",
    )(z, z, z, w, vec(b), vec(ln_g), vec(ln_b))


def _lru_kernel(*refs, rows, batch, ktaps, sub, reverse, final):
    if final:
        (zx_ref, zy_ref, hb_ref, cw_ref, cb_ref, wg_ref, ba_ref, bi_ref, lam_ref,
         o_ref, xs_ref, a_ref, u_ref, h_ref) = refs
    else:
        (zx_ref, cw_ref, cb_ref, wg_ref, ba_ref, bi_ref, lam_ref,
         o_ref, xs_ref, a_ref, u_ref, h_ref) = refs
    hl = (ktaps - 1) * batch
    main = 0 if reverse else hl
    ldim = zx_ref.shape[-1]
    ngrp = wg_ref.shape[0]
    gw = ldim // ngrp

    @pl.when(pl.program_id(0) == 0)
    def _():
        h_ref[...] = jnp.zeros_like(h_ref)
        xs_ref[...] = jnp.zeros_like(xs_ref)

    xs_ref[main:main + rows, :] = zx_ref[...].astype(F32)

    lam = lam_ref[...]
    sp = jnp.maximum(-lam, 0.0) + jnp.log(1.0 + jnp.exp(-jnp.abs(lam)))
    half_neg_c_sp = -0.5 * LRU_C * sp

    def gate_body(ci, carry):
        r0 = pl.multiple_of(ci * sub, sub)
        xc = jnp.broadcast_to(cb_ref[...], (sub, ldim))
        for k in range(ktaps):
            xc = xc + cw_ref[k:k + 1, :] * xs_ref[pl.ds(r0 + batch * k, sub), :]
        xcb = xc.astype(BF16)
        for g in range(ngrp):
            cols = slice(g * gw, (g + 1) * gw)
            gates = jnp.dot(xcb[:, cols], wg_ref[g], preferred_element_type=F32)
            tr = jnp.tanh(gates[:, :gw] + ba_ref[:, cols])
            ig2 = jnp.tanh(gates[:, gw:] + bi_ref[:, cols]) + 1.0
            log_a = (tr + 1.0) * half_neg_c_sp[:, cols]
            a_ref[pl.ds(r0, sub), cols] = jnp.exp(log_a)
            th = jnp.tanh(log_a)
            half_mult = jnp.sqrt(jnp.maximum(0.5 * th / (th - 1.0), 0.25e-12))
            u_ref[pl.ds(r0, sub), cols] = half_mult * (ig2 * xc[:, cols])
        return carry

    lax.fori_loop(0, rows // sub, gate_body, 0)

    if reverse:
        xs_ref[rows:rows + hl, :] = xs_ref[0:hl, :]
    else:
        xs_ref[0:hl, :] = xs_ref[rows:rows + hl, :]

    nt = rows // batch

    def scan_body(t, h):
        tt = (nt - 1 - t) if reverse else t
        r = pl.multiple_of(tt * batch, batch)
        h = a_ref[pl.ds(r, batch), :] * h + u_ref[pl.ds(r, batch), :]
        u_ref[pl.ds(r, batch), :] = h
        return h

    h_ref[...] = lax.fori_loop(0, nt, scan_body, h_ref[...], unroll=8)

    def out_body(ci, carry):
        r0 = pl.multiple_of(ci * sub, sub)
        h = u_ref[pl.ds(r0, sub), :]
        if final:
            h = h + hb_ref[pl.ds(r0, sub), :].astype(F32)
            h = _gelu_tanh(zy_ref[pl.ds(r0, sub), :].astype(F32)) * h
        o_ref[pl.ds(r0, sub), :] = h.astype(o_ref.dtype)
        return carry

    lax.fori_loop(0, rows // sub, out_body, 0)


def _pack_blockdiag(w, width):
    H, dh, _ = w.shape
    per = width // dh
    G = H // per
    w = w.reshape(G, per, dh, dh)
    eye = jnp.eye(per, dtype=w.dtype)
    return jnp.einsum("pq,gpij->gpiqj", eye, w).reshape(G, per * dh, per * dh)


def _lru_pass(z, hb, cw, cb, w_a, b_a, w_i, b_i, lam, *, batch, reverse, col_x, col_y):
    R = z.shape[0]
    ktaps, ldim = cw.shape
    final = hb is not None
    rows = _pick_tile(R, 1024)
    nblk = R // rows
    wg = (0.5 * jnp.concatenate([_pack_blockdiag(w_a, MXU_DIM), _pack_blockdiag(w_i, MXU_DIM)], axis=-1)).astype(BF16)
    b_a, b_i = 0.5 * b_a, 0.5 * b_i
    ngrp = wg.shape[0]
    vec = lambda a: a.reshape(1, ldim).astype(F32)
    blk = (lambda i: nblk - 1 - i) if reverse else (lambda i: i)
    const2 = lambda i: (0, 0)
    in_specs = [pl.BlockSpec((rows, ldim), lambda i: (blk(i), col_x))]
    args = [z]
    if final:
        in_specs += [pl.BlockSpec((rows, ldim), lambda i: (blk(i), col_y)),
                     pl.BlockSpec((rows, ldim), lambda i: (blk(i), 0))]
        args += [z, hb]
    in_specs += [
        pl.BlockSpec((ktaps, ldim), const2),
        pl.BlockSpec((1, ldim), const2),
        pl.BlockSpec((ngrp, MXU_DIM, 2 * MXU_DIM), lambda i: (0, 0, 0)),
        pl.BlockSpec((1, ldim), const2),
        pl.BlockSpec((1, ldim), const2),
        pl.BlockSpec((1, ldim), const2),
    ]
    args += [cw, vec(cb), wg, vec(b_a), vec(b_i), vec(lam)]
    hl = (ktaps - 1) * batch
    return pl.pallas_call(
        functools.partial(_lru_kernel, rows=rows, batch=batch, ktaps=ktaps, sub=min(rows, 256),
                          reverse=reverse, final=final),
        grid=(nblk,),
        in_specs=in_specs,
        out_specs=pl.BlockSpec((rows, ldim), lambda i: (blk(i), 0)),
        out_shape=jax.ShapeDtypeStruct((R, ldim), BF16),
        scratch_shapes=[
            pltpu.VMEM((rows + hl, ldim), F32),
            pltpu.VMEM((rows, ldim), F32),
            pltpu.VMEM((rows, ldim), F32),
            pltpu.VMEM((batch, ldim), F32),
        ],
        compiler_params=_params("arbitrary"),
        name="rglru_bwd" if reverse else "rglru_fwd",
    )(*args)


def _outproj_e_kernel(c_ref, r_ref, x_ref, pt_ref, w_ref, o_ref, m_ref):
    B, tt, _ = x_ref.shape
    tl = MXU_DIM // B
    cdim = c_ref.shape[1]
    for q in range(tt // tl):
        rows = slice(q * MXU_DIM, (q + 1) * MXU_DIM)
        m_ref[rows, :cdim] = jnp.dot(pt_ref[...], c_ref[rows, :], preferred_element_type=F32).astype(BF16)
        m_ref[rows, cdim:] = jnp.dot(pt_ref[...], r_ref[rows, :], preferred_element_type=F32).astype(BF16)
    acc = jnp.dot(m_ref[...], w_ref[...], preferred_element_type=F32)
    for q in range(tt // tl):
        for b in range(B):
            r0 = q * MXU_DIM + b * tl
            ts = slice(q * tl, (q + 1) * tl)
            o_ref[b, ts, :] = x_ref[b, ts, :] + acc[r0:r0 + tl, :]


def _outproj_e(c, r, x, w):
    B, S, D = x.shape
    cdim, ldim = c.shape[1], r.shape[1]
    tt = _pick_tile(S, 128)
    return pl.pallas_call(
        _outproj_e_kernel,
        grid=(S // tt,),
        in_specs=[
            pl.BlockSpec((tt * B, cdim), lambda i: (i, 0)),
            pl.BlockSpec((tt * B, ldim), lambda i: (i, 0)),
            pl.BlockSpec((B, tt, D), lambda i: (0, i, 0)),
            pl.BlockSpec((MXU_DIM, MXU_DIM), lambda i: (0, 0)),
            pl.BlockSpec((cdim + ldim, D), lambda i: (0, 0)),
        ],
        out_specs=pl.BlockSpec((B, tt, D), lambda i: (0, i, 0)),
        out_shape=jax.ShapeDtypeStruct((B, S, D), F32),
        scratch_shapes=[pltpu.VMEM((tt * B, cdim + ldim), BF16)],
        compiler_params=_params("parallel"),
        name="outproj_even",
    )(c, r, x, _perm_matrix(B).T, w)


FFN_COL_CHUNK = 2 * MXU_DIM


def _ffn_kernel(*refs, bounds):
    n_src = len(bounds) - 1
    x_refs = refs[:n_src]
    g_ref, w1_ref, w3_ref, w2_ref, o_ref, xn_ref = refs[n_src:]
    i = pl.program_id(0)
    for x_ref, lo, hi in zip(x_refs, bounds[:-1], bounds[1:]):
        @pl.when((i >= lo) & (i < hi))
        def _(x_ref=x_ref):
            x = x_ref[...]
            xn_ref[...] = _rmsnorm_f32(x, g_ref[...]).astype(BF16)
            o_ref[...] = x

    xn = xn_ref[...]
    F = w1_ref.shape[1]
    acc = None
    for c0 in range(0, F, FFN_COL_CHUNK):
        c1 = min(c0 + FFN_COL_CHUNK, F)
        a = jnp.dot(xn, w1_ref[:, c0:c1], preferred_element_type=F32)
        b = jnp.dot(xn, w3_ref[:, c0:c1], preferred_element_type=F32)
        act = (a * _sigmoid(a) * b).astype(BF16)
        part = jnp.dot(act, w2_ref[c0:c1, :], preferred_element_type=F32)
        acc = part if acc is None else acc + part
    o_ref[...] += acc


def _ffn(xs, g, w1, w3, w2, *, tm=1024):
    D = xs[0].shape[1]
    F = w1.shape[1]
    assert all(x.shape[0] % tm == 0 for x in xs)
    counts = [x.shape[0] // tm for x in xs]
    starts = [sum(counts[:k]) for k in range(len(xs))]
    T = tm * sum(counts)

    def src_spec(lo, n):
        return pl.BlockSpec((tm, D), lambda i: (jnp.clip(i - lo, 0, n - 1), 0))

    resident = dict(pipeline_mode=pl.Buffered(1))
    return pl.pallas_call(
        functools.partial(_ffn_kernel, bounds=tuple(starts) + (T // tm,)),
        grid=(T // tm,),
        in_specs=[src_spec(lo, n) for lo, n in zip(starts, counts)] + [
            pl.BlockSpec((1, D), lambda i: (0, 0)),
            pl.BlockSpec((D, F), lambda i: (0, 0), **resident),
            pl.BlockSpec((D, F), lambda i: (0, 0), **resident),
            pl.BlockSpec((F, D), lambda i: (0, 0), **resident),
        ],
        out_specs=pl.BlockSpec((tm, D), lambda i: (i, 0)),
        out_shape=jax.ShapeDtypeStruct((T, D), F32),
        scratch_shapes=[pltpu.VMEM((tm, D), BF16)],
        compiler_params=_params("parallel"),
        name="ffn_swiglu",
    )(*xs, g.reshape(1, D), w1, w3, w2)


def _inproj_o_kernel(x_ref, g_ref, w_ref, lg_ref, lb_ref, u_ref, v_ref, zv_ref, *, sdim, n_chunk):
    xn = _rmsnorm_f32(x_ref[...], g_ref[...]).astype(BF16)
    for j in range(sdim // n_chunk):
        cols = slice(j * n_chunk, (j + 1) * n_chunk)
        zu = _gelu_tanh(jnp.dot(xn, w_ref[:, cols], preferred_element_type=F32))
        u_ref[:, cols] = zu.astype(u_ref.dtype)
    for j in range(sdim // n_chunk):
        cols = slice(j * n_chunk, (j + 1) * n_chunk)
        wcols = slice(sdim + j * n_chunk, sdim + (j + 1) * n_chunk)
        zv_ref[:, cols] = _gelu_tanh(jnp.dot(xn, w_ref[:, wcols], preferred_element_type=F32))
    zv = zv_ref[...]
    mu = jnp.mean(zv, axis=-1, keepdims=True)
    d = zv - mu
    var = jnp.mean(d * d, axis=-1, keepdims=True)
    v_ref[...] = (d * lax.rsqrt(var + EPS) * lg_ref[...] + lb_ref[...]).astype(v_ref.dtype)


def _inproj_o(x, g, w, ln_g, ln_b):
    T, D = x.shape
    sdim = w.shape[1] // 2
    tm = _pick_tile(T, 512)
    row = lambda i: (i, 0)
    const = lambda i: (0, 0)
    return pl.pallas_call(
        functools.partial(_inproj_o_kernel, sdim=sdim, n_chunk=_pick_tile(sdim, 512)),
        grid=(T // tm,),
        scratch_shapes=[pltpu.VMEM((tm, sdim), F32)],
        in_specs=[
            pl.BlockSpec((tm, D), row),
            pl.BlockSpec((1, D), const),
            pl.BlockSpec((D, 2 * sdim), const),
            pl.BlockSpec((1, sdim), const),
            pl.BlockSpec((1, sdim), const),
        ],
        out_specs=[pl.BlockSpec((tm, sdim), row), pl.BlockSpec((tm, sdim), row)],
        out_shape=[jax.ShapeDtypeStruct((T, sdim), BF16), jax.ShapeDtypeStruct((T, sdim), BF16)],
        compiler_params=_params("parallel"),
        name="inproj_odd",
    )(x, g.reshape(1, D), w, ln_g.reshape(1, sdim), ln_b.reshape(1, sdim))


def _sgu_out_kernel(u_ref, v_ref, sw_ref, sb_ref, w_ref, x_ref, o_ref, g_ref, *, chunk, heads):
    tm, sdim = u_ref.shape
    hd = sdim // heads
    for n in range(tm // chunk):
        rows = slice(n * chunk, (n + 1) * chunk)
        for h in range(heads):
            lanes = slice(h * hd, (h + 1) * hd)
            sv = jnp.dot(sw_ref[h], v_ref[rows, lanes], preferred_element_type=F32) + sb_ref[h]
            g_ref[rows, lanes] = (u_ref[rows, lanes].astype(F32) * sv).astype(BF16)
    o_ref[...] = x_ref[...] + jnp.dot(g_ref[...], w_ref[...], preferred_element_type=F32)


def _sgu_out(u, v, sgu_w, sgu_b, w, x):
    T, sdim = u.shape
    D = x.shape[1]
    heads, chunk, _ = sgu_w.shape
    hd = sdim // heads
    tm = _pick_tile(T, 512)
    assert tm % chunk == 0
    sb = jnp.broadcast_to(sgu_b.astype(F32)[:, :, None], (heads, chunk, hd))
    row = lambda i: (i, 0)
    return pl.pallas_call(
        functools.partial(_sgu_out_kernel, chunk=chunk, heads=heads),
        grid=(T // tm,),
        in_specs=[
            pl.BlockSpec((tm, sdim), row),
            pl.BlockSpec((tm, sdim), row),
            pl.BlockSpec((heads, chunk, chunk), lambda i: (0, 0, 0)),
            pl.BlockSpec((heads, chunk, hd), lambda i: (0, 0, 0)),
            pl.BlockSpec((sdim, D), lambda i: (0, 0)),
            pl.BlockSpec((tm, D), row),
        ],
        out_specs=pl.BlockSpec((tm, D), row),
        out_shape=jax.ShapeDtypeStruct((T, D), F32),
        scratch_shapes=[pltpu.VMEM((tm, sdim), BF16)],
        compiler_params=_params("parallel"),
        name="sgu_outproj_odd",
    )(u, v, sgu_w, sb, w, x)


META_E1, META_E2, META_R1, META_R2, META_G1, META_G2 = range(6)
ROUTE_TILE = 512


def _router_kernel(x_ref, g_ref, wr_ref, meta_ref, route_ref, cnt_ref, run_ref, *, n_experts):
    @pl.when(pl.program_id(0) == 0)
    def _():
        run_ref[...] = jnp.zeros_like(run_ref)

    xn = _rmsnorm_f32(x_ref[...], g_ref[...])
    xh = xn.astype(BF16)
    xl = (xn - xh.astype(F32)).astype(BF16)
    logits = (jnp.dot(xh, wr_ref[0], preferred_element_type=F32)
              + jnp.dot(xl, wr_ref[0], preferred_element_type=F32)
              + jnp.dot(xh, wr_ref[1], preferred_element_type=F32))
    tm = logits.shape[0]
    lane = lax.broadcasted_iota(jnp.int32, logits.shape, 1).astype(F32)
    neg = jnp.float32(-jnp.inf)
    big = jnp.float32(1e9)
    l1 = jnp.where(lane < n_experts, logits, neg)
    m1 = jnp.max(l1, axis=-1, keepdims=True)
    i1 = jnp.min(jnp.where(l1 == m1, lane, big), axis=-1, keepdims=True)
    l2 = jnp.where(lane == i1, neg, l1)
    m2 = jnp.max(l2, axis=-1, keepdims=True)
    i2 = jnp.min(jnp.where(l2 == m2, lane, big), axis=-1, keepdims=True)
    e2 = jnp.exp(m2 - m1)
    den = 1.0 + e2
    g1 = 1.0 / den
    g2 = e2 / den

    oh1 = jnp.where(lane == i1, 1.0, 0.0)
    oh2 = jnp.where(lane == i2, 1.0, 0.0)
    cnt = oh1 + oh2
    ri = lax.broadcasted_iota(jnp.int32, (tm, tm), 0)
    ci = lax.broadcasted_iota(jnp.int32, (tm, tm), 1)
    lower = jnp.where(ci < ri, 1.0, 0.0).astype(BF16)
    prefix = jnp.dot(lower, cnt.astype(BF16), preferred_element_type=F32) + run_ref[...]
    r1 = jnp.sum(oh1 * prefix, axis=-1, keepdims=True)
    r2 = jnp.sum(oh2 * prefix, axis=-1, keepdims=True)
    run_ref[...] += jnp.sum(cnt, axis=0, keepdims=True)

    meta = jnp.zeros_like(logits)
    for idx, val in ((META_E1, i1), (META_E2, i2), (META_R1, r1), (META_R2, r2), (META_G1, g1), (META_G2, g2)):
        meta = jnp.where(lane == idx, val, meta)
    meta_ref[...] = meta
    route_ref[...] = meta.T[:route_ref.shape[0], :]
    cnt_ref[...] = jnp.broadcast_to(run_ref[...], cnt_ref.shape)


def _router(x, g, w_router):
    T, D = x.shape
    E = w_router.shape[1]
    wr = jnp.zeros((D, LANES), F32).at[:, :E].set(w_router.astype(F32))
    wr_hi = wr.astype(BF16)
    wr = jnp.stack([wr_hi, (wr - wr_hi.astype(F32)).astype(BF16)])
    tm = ROUTE_TILE
    assert T % tm == 0
    row = lambda i: (i, 0)
    return pl.pallas_call(
        functools.partial(_router_kernel, n_experts=E),
        grid=(T // tm,),
        in_specs=[
            pl.BlockSpec((tm, D), row),
            pl.BlockSpec((1, D), lambda i: (0, 0)),
            pl.BlockSpec((2, D, LANES), lambda i: (0, 0, 0)),
        ],
        out_specs=[pl.BlockSpec((tm, LANES), row), pl.BlockSpec((SUBLANES, tm), lambda i: (0, i)),
                   pl.BlockSpec((SUBLANES, LANES), lambda i: (0, 0))],
        out_shape=[jax.ShapeDtypeStruct((T, LANES), F32), jax.ShapeDtypeStruct((SUBLANES, T), F32),
                   jax.ShapeDtypeStruct((SUBLANES, LANES), F32)],
        scratch_shapes=[pltpu.VMEM((1, LANES), F32)],
        compiler_params=_params("arbitrary"),
        name="moe_router",
    )(x, g.reshape(1, D), wr)


def _route_plan(route, cnt, n_experts, tmx, n_tiles):
    counts = cnt[0, :n_experts].astype(jnp.int32)
    nt = (counts + tmx - 1) // tmx
    tend = jnp.cumsum(nt)
    off = (tend - nt) * tmx
    eids = jnp.arange(n_experts, dtype=jnp.int32)

    def pos(e_row, r_row):
        e = route[e_row].astype(jnp.int32)
        base = jnp.sum(jnp.where(e[None, :] == eids[:, None], off[:, None], 0), axis=0)
        return base + route[r_row].astype(jnp.int32)

    T = route.shape[1]
    tm = ROUTE_TILE
    p1 = pos(META_E1, META_R1).reshape(T // tm, 1, tm)
    p2 = pos(META_E2, META_R2).reshape(T // tm, 1, tm)
    tiles = jnp.arange(n_tiles, dtype=jnp.int32)
    tile_expert = jnp.minimum(jnp.sum((tend[None, :] <= tiles[:, None]).astype(jnp.int32), axis=1), n_experts - 1)
    plan = dict(pos=jnp.concatenate([p1, p2], axis=2), tile_expert=tile_expert.astype(jnp.int32),
                n_active=tend[-1:].astype(jnp.int32), pad_start=(off + counts).astype(jnp.int32),
                pad_len=(nt * tmx - counts).astype(jnp.int32))
    return plan


def _dispatch_kernel(pstart_ref, plen_ref, na_ref, pos_ref, x_ref, g_ref, w1_ref, w3_ref, w2_ref,
                     xs_ref, w1o_ref, w3o_ref, w2o_ref, xn_ref, z_ref, sem, zsem,
                     *, pad_bits, n_steps, tmx):
    i = pl.program_id(0)
    tm = x_ref.shape[0]
    slot = i % 2

    third = n_steps // 3
    for k, (src_ref, dst_ref) in enumerate(((w1_ref, w1o_ref), (w3_ref, w3o_ref), (w2_ref, w2o_ref))):
        @pl.when((i >= k * third) & (i < (k + 1) * third))
        def _(src_ref=src_ref, dst_ref=dst_ref):
            dst_ref[...] = src_ref[...].astype(dst_ref.dtype)

    @pl.when(i == 0)
    def _():
        z_ref[...] = jnp.zeros_like(z_ref)

        def zero_rows(dst, rows):
            cp = pltpu.make_async_copy(z_ref.at[pl.ds(0, rows)], xs_ref.at[pl.ds(dst, rows)], zsem)
            cp.start()
            cp.wait()

        for e in range(pstart_ref.shape[0]):
            n = plen_ref[e]
            s = pstart_ref[e]
            head = n & (SUBLANES - 1)
            for j in range(SUBLANES - 1):
                @pl.when(j < head)
                def _(j=j, s=s):
                    zero_rows(s + j, 1)

            groups = n >> 3
            base = s + head
            for k in reversed(range(pad_bits - 3)):
                @pl.when(((groups >> k) & 1) == 1)
                def _(k=k, groups=groups, base=base):
                    dst = pl.multiple_of(base + (((groups >> (k + 1)) << (k + 1)) << 3), SUBLANES)
                    zero_rows(dst, SUBLANES << k)

        zrows = z_ref.shape[0]
        n_tiles = xs_ref.shape[0] // tmx
        for t in range(n_tiles - pstart_ref.shape[0], n_tiles):
            @pl.when(t >= na_ref[0])
            def _(t=t):
                for c in range(tmx // zrows):
                    zero_rows(t * tmx + c * zrows, zrows)

    xn = _rmsnorm_f32(x_ref[...], g_ref[...])

    for s in range(2):
        @pl.when(slot == s)
        def _(s=s):
            xn_ref[s] = xn

            def issue(q, carry):
                r0 = pl.multiple_of(q * SUBLANES, SUBLANES)
                for j in range(SUBLANES):
                    src = xn_ref.at[s, pl.ds(r0 + j, 1)]
                    pltpu.make_async_copy(src, xs_ref.at[pl.ds(pos_ref[0, r0 + j], 1)], sem.at[s]).start()
                    pltpu.make_async_copy(src, xs_ref.at[pl.ds(pos_ref[0, tm + r0 + j], 1)], sem.at[s]).start()
                return carry

            lax.fori_loop(0, tm // SUBLANES, issue, 0)

    def drain(s):
        for _ in range(2):
            pltpu.make_async_copy(xn_ref.at[s], xs_ref.at[pl.ds(0, tm)], sem.at[s]).wait()

    @pl.when(i > 0)
    def _():
        drain(1 - slot)

    @pl.when(i == n_steps - 1)
    def _():
        drain(slot)


def _dispatch(x, g, plan, n_rows, tmx, w1, w3, w2):
    T, D = x.shape
    tm = ROUTE_TILE
    n = T // tm
    pad_bits = (tmx - 1).bit_length()
    assert n_rows % tmx == 0 and tmx % (1 << (pad_bits - 1)) == 0
    E, _, F = w1.shape
    third = n // 3
    assert n % 3 == 0 and (E * D) % third == 0 and (E * F) % third == 0
    r13, r2 = (E * D) // third, (E * F) // third
    assert r13 % (2 * SUBLANES) == 0 and r2 % (2 * SUBLANES) == 0

    def slab(rows, width, k):
        return pl.BlockSpec((rows, width), lambda i, ps, pn, na: (jnp.clip(i - k * third, 0, third - 1), 0))

    grid_spec = pltpu.PrefetchScalarGridSpec(
        num_scalar_prefetch=3,
        grid=(n,),
        in_specs=[
            pl.BlockSpec((None, 1, 2 * tm), lambda i, ps, pn, na: (i, 0, 0), memory_space=pltpu.SMEM),
            pl.BlockSpec((tm, D), lambda i, ps, pn, na: (i, 0)),
            pl.BlockSpec((1, D), lambda i, ps, pn, na: (0, 0)),
            slab(r13, F, 0), slab(r13, F, 1), slab(r2, D, 2),
        ],
        out_specs=[pl.BlockSpec(memory_space=pl.ANY), slab(r13, F, 0), slab(r13, F, 1), slab(r2, D, 2)],
        scratch_shapes=[
            pltpu.VMEM((2, tm, D), F32),
            pltpu.VMEM((1 << (pad_bits - 1), D), F32),
            pltpu.SemaphoreType.DMA((2,)),
            pltpu.SemaphoreType.DMA(()),
        ],
    )
    xs, w1b, w3b, w2b = pl.pallas_call(
        functools.partial(_dispatch_kernel, pad_bits=pad_bits, n_steps=n, tmx=tmx),
        grid_spec=grid_spec,
        out_shape=[jax.ShapeDtypeStruct((n_rows, D), F32), jax.ShapeDtypeStruct((E * D, F), BF16),
                   jax.ShapeDtypeStruct((E * D, F), BF16), jax.ShapeDtypeStruct((E * F, D), BF16)],
        compiler_params=_params("arbitrary"),
        name="moe_dispatch",
    )(plan["pad_start"], plan["pad_len"], plan["n_active"], plan["pos"], x, g.reshape(1, D),
      w1.reshape(E * D, F), w3.reshape(E * D, F), w2.reshape(E * F, D))
    return xs, w1b.reshape(E, D, F), w3b.reshape(E, D, F), w2b.reshape(E, F, D)


MOE_COL_CHUNK = 4 * MXU_DIM


def _moe_experts_kernel(te_ref, na_ref, x_ref, w1_ref, w3_ref, w2_ref, o_ref, xb_ref):
    del te_ref
    f = pl.program_id(1)
    active = pl.program_id(0) < na_ref[0]

    @pl.when(f == 0)
    def _():
        o_ref[...] = jnp.zeros_like(o_ref)

    @pl.when((f == 0) & active)
    def _():
        xb_ref[...] = x_ref[...].astype(BF16)

    @pl.when(active)
    def _():
        xn = xb_ref[...]
        tf = w1_ref.shape[1]
        acc = None
        for c0 in range(0, tf, MOE_COL_CHUNK):
            c1 = min(c0 + MOE_COL_CHUNK, tf)
            a = jnp.dot(xn, w1_ref[:, c0:c1], preferred_element_type=F32)
            b = jnp.dot(xn, w3_ref[:, c0:c1], preferred_element_type=F32)
            act = (a * _sigmoid(a) * b).astype(BF16)
            part = jnp.dot(act, w2_ref[c0:c1, :], preferred_element_type=F32)
            acc = part if acc is None else acc + part
        o_ref[...] += acc


def _moe_experts(xs, tile_expert, n_active, w1, w3, w2, *, tmx, tf):
    n_rows, D = xs.shape
    E, _, F = w1.shape
    assert F % tf == 0 and n_rows % tmx == 0
    nf = F // tf
    fsel = lambda i, f, na: jnp.where(i < na[0], f, nf - 1)
    grid_spec = pltpu.PrefetchScalarGridSpec(
        num_scalar_prefetch=2,
        grid=(n_rows // tmx, nf),
        in_specs=[
            pl.BlockSpec((tmx, D), lambda i, f, te, na: (jnp.minimum(i, na[0] - 1), 0)),
            pl.BlockSpec((None, D, tf), lambda i, f, te, na: (te[i], 0, fsel(i, f, na))),
            pl.BlockSpec((None, D, tf), lambda i, f, te, na: (te[i], 0, fsel(i, f, na))),
            pl.BlockSpec((None, tf, D), lambda i, f, te, na: (te[i], fsel(i, f, na), 0)),
        ],
        out_specs=pl.BlockSpec((tmx, D), lambda i, f, te, na: (i, 0)),
        scratch_shapes=[pltpu.VMEM((tmx, D), BF16)],
    )
    return pl.pallas_call(
        _moe_experts_kernel,
        grid_spec=grid_spec,
        out_shape=jax.ShapeDtypeStruct((n_rows, D), F32),
        compiler_params=_params("parallel", "arbitrary"),
        name="moe_experts",
    )(tile_expert, n_active, xs, w1, w3, w2)


def _combine_kernel(pos_ref, posn_ref, meta_ref, res_ref, lnf_ref, y_ref, o_ref, ya_ref, yb_ref, sem, *, n_steps):
    i = pl.program_id(0)
    tm = res_ref.shape[0]
    slot = i % 2

    def gather(p_ref, s):
        def issue(q, carry):
            r0 = pl.multiple_of(q * SUBLANES, SUBLANES)
            for j in range(SUBLANES):
                r = r0 + j
                pltpu.make_async_copy(y_ref.at[pl.ds(p_ref[0, r], 1)], ya_ref.at[s, pl.ds(r, 1)], sem.at[s]).start()
                pltpu.make_async_copy(y_ref.at[pl.ds(p_ref[0, tm + r], 1)], yb_ref.at[s, pl.ds(r, 1)],
                                      sem.at[s]).start()
            return carry

        lax.fori_loop(0, tm // SUBLANES, issue, 0)

    @pl.when(i == 0)
    def _():
        gather(pos_ref, 0)

    for s in range(2):
        @pl.when((i + 1 < n_steps) & (slot == 1 - s))
        def _(s=s):
            gather(posn_ref, s)

    pltpu.make_async_copy(y_ref.at[pl.ds(0, tm)], ya_ref.at[slot], sem.at[slot]).wait()
    pltpu.make_async_copy(y_ref.at[pl.ds(0, tm)], yb_ref.at[slot], sem.at[slot]).wait()

    meta = meta_ref[...]
    g1 = meta[:, META_G1:META_G1 + 1]
    g2 = meta[:, META_G2:META_G2 + 1]
    h = res_ref[...] + g1 * ya_ref[slot] + g2 * yb_ref[slot]
    o_ref[...] = _rmsnorm_f32(h, lnf_ref[...])


def _combine(y, pos, meta, res, ln_final, t_len, t_off):
    D = res.shape[1]
    tm = ROUTE_TILE
    assert t_len % tm == 0 and t_off % tm == 0
    off = t_off // tm
    last = off + t_len // tm - 1
    return pl.pallas_call(
        functools.partial(_combine_kernel, n_steps=t_len // tm),
        grid=(t_len // tm,),
        in_specs=[
            pl.BlockSpec((None, 1, 2 * tm), lambda i: (off + i, 0, 0), memory_space=pltpu.SMEM),
            pl.BlockSpec((None, 1, 2 * tm), lambda i: (jnp.minimum(off + i + 1, last), 0, 0),
                         memory_space=pltpu.SMEM),
            pl.BlockSpec((tm, LANES), lambda i: (off + i, 0)),
            pl.BlockSpec((tm, D), lambda i: (off + i, 0)),
            pl.BlockSpec((1, D), lambda i: (0, 0)),
            pl.BlockSpec(memory_space=pl.ANY),
        ],
        out_specs=pl.BlockSpec((tm, D), lambda i: (i, 0)),
        out_shape=jax.ShapeDtypeStruct((t_len, D), F32),
        scratch_shapes=[pltpu.VMEM((2, tm, D), F32), pltpu.VMEM((2, tm, D), F32), pltpu.SemaphoreType.DMA((2,))],
        compiler_params=_params("arbitrary"),
        name="moe_combine",
    )(pos, pos, meta, res, ln_final.reshape(1, D), y)


MOE_ROW_TILE = 1024
MOE_FF_TILE = 1792


def _even_mixer(x, p):
    B = x.shape[0]
    cdim = p["dw_conv_w"].shape[-1]
    ldim = p["rg_lam"].shape[-1]
    assert cdim * 2 == ldim and p["w_in_e"].shape[-1] == 2 * cdim + 2 * ldim
    z = _inproj_e(x, p["ln_mix_e"], p["w_in_e"])
    c = _conv_branch(z, p["dw_conv_w"], p["dw_conv_b"], p["conv_ln_g"], p["conv_ln_b"], B)
    lru = functools.partial(_lru_pass, z, batch=B, col_x=2, col_y=1)
    hb = lru(None, p["rg_conv_w"][1], p["rg_conv_b"][1], p["rg_w_a"][1], p["rg_b_a"][1],
             p["rg_w_i"][1], p["rg_b_i"][1], p["rg_lam"][1], reverse=True)
    r = lru(hb, p["rg_conv_w"][0], p["rg_conv_b"][0], p["rg_w_a"][0], p["rg_b_a"][0],
            p["rg_w_i"][0], p["rg_b_i"][0], p["rg_lam"][0], reverse=False)
    return _outproj_e(c, r, x, p["w_out_e"])


def _trunks(xs, p):
    D = xs[0].shape[-1]
    lens = [x.shape[0] * x.shape[1] for x in xs]
    offs = [sum(lens[:k]) for k in range(len(xs))]
    T = sum(lens)

    h = _ffn([_even_mixer(x, p).reshape(n, D) for x, n in zip(xs, lens)],
             p["ln_ffn_e"], p["ffn_w1"], p["ffn_w3"], p["ffn_w2"])

    u, v = _inproj_o(h, p["ln_mix_o"], p["w_in_o"], p["sgu_ln_g"], p["sgu_ln_b"])
    h = _sgu_out(u, v, p["sgu_w"], p["sgu_b"], p["w_out_o"], h)

    E = p["w_router"].shape[1]
    tmx = MOE_ROW_TILE
    tf = MOE_FF_TILE if p["moe_w1"].shape[-1] % MOE_FF_TILE == 0 else _pick_tile(p["moe_w1"].shape[-1], 512)
    n_tiles = (2 * T) // tmx + E
    meta, route, cnt = _router(h, p["ln_ffn_o"], p["w_router"])
    plan = _route_plan(route, cnt, E, tmx, n_tiles)
    xsort, w1, w3, w2 = _dispatch(h, p["ln_ffn_o"], plan, n_tiles * tmx, tmx,
                                  p["moe_w1"], p["moe_w3"], p["moe_w2"])
    y = _moe_experts(xsort, plan["tile_expert"], plan["n_active"], w1, w3, w2, tmx=tmx, tf=tf)
    return tuple(_combine(y, plan["pos"], meta, h, p["ln_final"], n, off).reshape(x.shape)
                 for x, n, off in zip(xs, lens, offs))


def kernel(x_prompt, x_sample, ln_mix_e, w_in_e, dw_conv_w, dw_conv_b, conv_ln_g, conv_ln_b, rg_conv_w, rg_conv_b, rg_w_a, rg_b_a, rg_w_i, rg_b_i, rg_lam, w_out_e, ln_ffn_e, ffn_w1, ffn_w3, ffn_w2, ln_mix_o, w_in_o, sgu_ln_g, sgu_ln_b, sgu_w, sgu_b, w_out_o, ln_ffn_o, w_router, moe_w1, moe_w3, moe_w2, ln_final):
    assert ln_mix_e.shape[0] == 1 and ln_mix_o.shape[0] == 1, "one even and one odd layer"
    bf = lambda a: a[0].astype(BF16)
    f32 = lambda a: a[0].astype(F32)
    p = {
        "ln_mix_e": f32(ln_mix_e), "w_in_e": bf(w_in_e),
        "dw_conv_w": f32(dw_conv_w), "dw_conv_b": f32(dw_conv_b),
        "conv_ln_g": f32(conv_ln_g), "conv_ln_b": f32(conv_ln_b),
        "rg_conv_w": f32(rg_conv_w), "rg_conv_b": f32(rg_conv_b),
        "rg_w_a": f32(rg_w_a), "rg_b_a": f32(rg_b_a), "rg_w_i": f32(rg_w_i), "rg_b_i": f32(rg_b_i),
        "rg_lam": f32(rg_lam), "w_out_e": bf(w_out_e),
        "ln_ffn_e": f32(ln_ffn_e), "ffn_w1": bf(ffn_w1), "ffn_w3": bf(ffn_w3), "ffn_w2": bf(ffn_w2),
        "ln_mix_o": f32(ln_mix_o), "w_in_o": bf(w_in_o),
        "sgu_ln_g": f32(sgu_ln_g), "sgu_ln_b": f32(sgu_ln_b),
        "sgu_w": bf(sgu_w), "sgu_b": f32(sgu_b), "w_out_o": bf(w_out_o),
        "ln_ffn_o": f32(ln_ffn_o), "w_router": f32(w_router),
        "moe_w1": f32(moe_w1), "moe_w3": f32(moe_w3), "moe_w2": f32(moe_w2),
        "ln_final": ln_final.astype(F32),
    }
    return _trunks([x_prompt, x_sample], p)
```

```python
import functools

import jax
import jax.numpy as jnp
from jax import lax
from jax.experimental import pallas as pl
from jax.experimental.pallas import tpu as pltpu

EPS = 1e-6
LRU_C = 8.0
LANES = 128
SUBLANES = 8
MXU_DIM = 256
VMEM_LIMIT_BYTES = 56 * 1024 * 1024
F32 = jnp.float32
BF16 = jnp.bfloat16


def _params(*sem):
    return pltpu.CompilerParams(dimension_semantics=sem, vmem_limit_bytes=VMEM_LIMIT_BYTES)


def _sigmoid(x):
    return 0.5 * (jnp.tanh(0.5 * x) + 1.0)


def _gelu_tanh(x):
    c = 0.7978845608028654
    hx = 0.5 * x
    return hx + hx * jnp.tanh(x * (c + (c * 0.044715) * (x * x)))


def _rmsnorm_f32(x, g):
    return x * lax.rsqrt(jnp.mean(x * x, axis=-1, keepdims=True) + EPS) * g


def _pick_tile(n, want):
    t = min(n, want)
    while n % t:
        t //= 2
    return t


def _perm_matrix(batch):
    tl = MXU_DIM // batch
    r = jnp.arange(MXU_DIM)
    src = (r % batch) * tl + r // batch
    return (src[:, None] == r[None, :]).astype(BF16)


def _inproj_e_kernel(x_ref, g_ref, p_ref, w_ref, o_ref, xt_ref, *, n_chunk):
    B, tt, _ = x_ref.shape
    tl = MXU_DIM // B
    for q in range(tt // tl):
        xb = jnp.concatenate([x_ref[b, q * tl:(q + 1) * tl, :] for b in range(B)], axis=0)
        xn = _rmsnorm_f32(xb, g_ref[...]).astype(BF16)
        xt_ref[q * MXU_DIM:(q + 1) * MXU_DIM, :] = jnp.dot(
            p_ref[...], xn, preferred_element_type=F32).astype(BF16)
    xt = xt_ref[...]
    n = o_ref.shape[-1]
    for j in range(n // n_chunk):
        cols = slice(j * n_chunk, (j + 1) * n_chunk)
        o_ref[:, cols] = jnp.dot(xt, w_ref[:, cols], preferred_element_type=F32).astype(o_ref.dtype)


def _inproj_e(x, g, w):
    B, S, D = x.shape
    N = w.shape[1]
    assert MXU_DIM % B == 0
    tt = _pick_tile(S, 128)
    assert tt % (MXU_DIM // B) == 0
    return pl.pallas_call(
        functools.partial(_inproj_e_kernel, n_chunk=1024),
        grid=(S // tt,),
        in_specs=[
            pl.BlockSpec((B, tt, D), lambda i: (0, i, 0)),
            pl.BlockSpec((1, D), lambda i: (0, 0)),
            pl.BlockSpec((MXU_DIM, MXU_DIM), lambda i: (0, 0)),
            pl.BlockSpec((D, N), lambda i: (0, 0)),
        ],
        out_specs=pl.BlockSpec((tt * B, N), lambda i: (i, 0)),
        out_shape=jax.ShapeDtypeStruct((S * B, N), BF16),
        scratch_shapes=[pltpu.VMEM((tt * B, D), BF16)],
        compiler_params=_params("parallel"),
        name="inproj_even",
    )(x, g.reshape(1, D), _perm_matrix(B), w)


def _conv_kernel(zp_ref, zm_ref, zn_ref, w_ref, b_ref, g_ref, be_ref, o_ref, s_ref, y_ref,
                 *, rows, halo, cdim, ktaps, batch, unroll, ln_rows, n_steps):
    i = pl.program_id(0)
    n = n_steps

    def glu(z):
        return z[:, :cdim].astype(F32) * _sigmoid(z[:, cdim:].astype(F32))

    s_ref[halo:halo + rows, :] = glu(zm_ref[...])
    s_ref[0:halo, :] = jnp.where(i > 0, glu(zp_ref[...]), 0.0)
    s_ref[halo + rows:, :] = jnp.where(i < n - 1, glu(zn_ref[...]), 0.0)

    pad = ktaps // 2
    base = halo - pad * batch
    step = batch * unroll
    for j in range(cdim // LANES):
        lanes = slice(j * LANES, (j + 1) * LANES)
        wj = [jnp.broadcast_to(w_ref[k:k + 1, lanes], (batch, LANES)) for k in range(ktaps)]
        bj = jnp.broadcast_to(b_ref[:, lanes], (batch, LANES))

        def body(ci, carry, lanes=lanes, wj=wj, bj=bj):
            r0 = pl.multiple_of(ci * step, step)
            accs = [bj] * unroll
            for m in range(unroll + ktaps - 1):
                xm = s_ref[pl.ds(r0 + base + batch * m, batch), lanes]
                for u in range(unroll):
                    k = m - u
                    if 0 <= k < ktaps:
                        accs[u] = accs[u] + wj[k] * xm
            for u in range(unroll):
                y_ref[pl.ds(r0 + batch * u, batch), lanes] = accs[u]
            return carry

        lax.fori_loop(0, rows // step, body, 0)

    def ln_body(ci, carry):
        r0 = pl.multiple_of(ci * ln_rows, ln_rows)
        y = y_ref[pl.ds(r0, ln_rows), :]
        mu = jnp.mean(y, axis=-1, keepdims=True)
        d = y - mu
        var = jnp.mean(d * d, axis=-1, keepdims=True)
        yn = d * lax.rsqrt(var + EPS) * g_ref[...] + be_ref[...]
        o_ref[pl.ds(r0, ln_rows), :] = (yn * _sigmoid(yn)).astype(o_ref.dtype)
        return carry

    lax.fori_loop(0, rows // ln_rows, ln_body, 0)


def _conv_branch(z, w, b, ln_g, ln_b, batch):
    R = z.shape[0]
    ktaps, cdim = w.shape
    halo = 128
    assert (ktaps // 2) * batch <= halo and batch == SUBLANES
    rows = _pick_tile(R, 2048)
    assert rows % halo == 0
    hb = rows // halo
    nh = R // halo
    vec = lambda a: a.reshape(1, cdim)
    return pl.pallas_call(
        functools.partial(_conv_kernel, rows=rows, halo=halo, cdim=cdim, ktaps=ktaps, batch=batch,
                          unroll=16, ln_rows=min(rows, 1024), n_steps=R // rows),
        grid=(R // rows,),
        in_specs=[
            pl.BlockSpec((halo, 2 * cdim), lambda i: (jnp.maximum(i * hb - 1, 0), 0)),
            pl.BlockSpec((rows, 2 * cdim), lambda i: (i, 0)),
            pl.BlockSpec((halo, 2 * cdim), lambda i: (jnp.minimum((i + 1) * hb, nh - 1), 0)),
            pl.BlockSpec((ktaps, cdim), lambda i: (0, 0)),
            pl.BlockSpec((1, cdim), lambda i: (0, 0)),
            pl.BlockSpec((1, cdim), lambda i: (0, 0)),
            pl.BlockSpec((1, cdim), lambda i: (0, 0)),
        ],
        out_specs=pl.BlockSpec((rows, cdim), lambda i: (i, 0)),
        out_shape=jax.ShapeDtypeStruct((R, cdim), BF16),
        scratch_shapes=[pltpu.VMEM((rows + 2 * halo, cdim), F32), pltpu.VMEM((rows, cdim), F32)],
        compiler_params=_params("parallel"),
        name="conv_branch",
    )(z, z, z, w, vec(b), vec(ln_g), vec(ln_b))


def _lru_kernel(*refs, rows, batch, ktaps, sub, reverse, final):
    if final:
        (zx_ref, zy_ref, hb_ref, cw_ref, cb_ref, wg_ref, ba_ref, bi_ref, lam_ref,
         o_ref, xs_ref, a_ref, u_ref, h_ref) = refs
    else:
        (zx_ref, cw_ref, cb_ref, wg_ref, ba_ref, bi_ref, lam_ref,
         o_ref, xs_ref, a_ref, u_ref, h_ref) = refs
    hl = (ktaps - 1) * batch
    main = 0 if reverse else hl
    ldim = zx_ref.shape[-1]
    ngrp = wg_ref.shape[0]
    gw = ldim // ngrp

    @pl.when(pl.program_id(0) == 0)
    def _():
        h_ref[...] = jnp.zeros_like(h_ref)
        xs_ref[...] = jnp.zeros_like(xs_ref)

    xs_ref[main:main + rows, :] = zx_ref[...].astype(F32)

    lam = lam_ref[...]
    sp = jnp.maximum(-lam, 0.0) + jnp.log(1.0 + jnp.exp(-jnp.abs(lam)))
    half_neg_c_sp = -0.5 * LRU_C * sp

    def gate_body(ci, carry):
        r0 = pl.multiple_of(ci * sub, sub)
        xc = jnp.broadcast_to(cb_ref[...], (sub, ldim))
        for k in range(ktaps):
            xc = xc + cw_ref[k:k + 1, :] * xs_ref[pl.ds(r0 + batch * k, sub), :]
        xcb = xc.astype(BF16)
        for g in range(ngrp):
            cols = slice(g * gw, (g + 1) * gw)
            gates = jnp.dot(xcb[:, cols], wg_ref[g], preferred_element_type=F32)
            tr = jnp.tanh(gates[:, :gw] + ba_ref[:, cols])
            ig2 = jnp.tanh(gates[:, gw:] + bi_ref[:, cols]) + 1.0
            log_a = (tr + 1.0) * half_neg_c_sp[:, cols]
            a_ref[pl.ds(r0, sub), cols] = jnp.exp(log_a)
            th = jnp.tanh(log_a)
            half_mult = jnp.sqrt(jnp.maximum(0.5 * th / (th - 1.0), 0.25e-12))
            u_ref[pl.ds(r0, sub), cols] = half_mult * (ig2 * xc[:, cols])
        return carry

    lax.fori_loop(0, rows // sub, gate_body, 0)

    if reverse:
        xs_ref[rows:rows + hl, :] = xs_ref[0:hl, :]
    else:
        xs_ref[0:hl, :] = xs_ref[rows:rows + hl, :]

    nt = rows // batch

    def scan_body(t, h):
        tt = (nt - 1 - t) if reverse else t
        r = pl.multiple_of(tt * batch, batch)
        h = a_ref[pl.ds(r, batch), :] * h + u_ref[pl.ds(r, batch), :]
        u_ref[pl.ds(r, batch), :] = h
        return h

    h_ref[...] = lax.fori_loop(0, nt, scan_body, h_ref[...], unroll=8)

    def out_body(ci, carry):
        r0 = pl.multiple_of(ci * sub, sub)
        h = u_ref[pl.ds(r0, sub), :]
        if final:
            h = h + hb_ref[pl.ds(r0, sub), :].astype(F32)
            h = _gelu_tanh(zy_ref[pl.ds(r0, sub), :].astype(F32)) * h
        o_ref[pl.ds(r0, sub), :] = h.astype(o_ref.dtype)
        return carry

    lax.fori_loop(0, rows // sub, out_body, 0)


def _pack_blockdiag(w, width):
    H, dh, _ = w.shape
    per = width // dh
    G = H // per
    w = w.reshape(G, per, dh, dh)
    eye = jnp.eye(per, dtype=w.dtype)
    return jnp.einsum("pq,gpij->gpiqj", eye, w).reshape(G, per * dh, per * dh)


def _lru_pass(z, hb, cw, cb, w_a, b_a, w_i, b_i, lam, *, batch, reverse, col_x, col_y):
    R = z.shape[0]
    ktaps, ldim = cw.shape
    final = hb is not None
    rows = _pick_tile(R, 1024)
    nblk = R // rows
    wg = (0.5 * jnp.concatenate([_pack_blockdiag(w_a, MXU_DIM), _pack_blockdiag(w_i, MXU_DIM)], axis=-1)).astype(BF16)
    b_a, b_i = 0.5 * b_a, 0.5 * b_i
    ngrp = wg.shape[0]
    vec = lambda a: a.reshape(1, ldim).astype(F32)
    blk = (lambda i: nblk - 1 - i) if reverse else (lambda i: i)
    const2 = lambda i: (0, 0)
    in_specs = [pl.BlockSpec((rows, ldim), lambda i: (blk(i), col_x))]
    args = [z]
    if final:
        in_specs += [pl.BlockSpec((rows, ldim), lambda i: (blk(i), col_y)),
                     pl.BlockSpec((rows, ldim), lambda i: (blk(i), 0))]
        args += [z, hb]
    in_specs += [
        pl.BlockSpec((ktaps, ldim), const2),
        pl.BlockSpec((1, ldim), const2),
        pl.BlockSpec((ngrp, MXU_DIM, 2 * MXU_DIM), lambda i: (0, 0, 0)),
        pl.BlockSpec((1, ldim), const2),
        pl.BlockSpec((1, ldim), const2),
        pl.BlockSpec((1, ldim), const2),
    ]
    args += [cw, vec(cb), wg, vec(b_a), vec(b_i), vec(lam)]
    hl = (ktaps - 1) * batch
    return pl.pallas_call(
        functools.partial(_lru_kernel, rows=rows, batch=batch, ktaps=ktaps, sub=min(rows, 256),
                          reverse=reverse, final=final),
        grid=(nblk,),
        in_specs=in_specs,
        out_specs=pl.BlockSpec((rows, ldim), lambda i: (blk(i), 0)),
        out_shape=jax.ShapeDtypeStruct((R, ldim), BF16),
        scratch_shapes=[
            pltpu.VMEM((rows + hl, ldim), F32),
            pltpu.VMEM((rows, ldim), F32),
            pltpu.VMEM((rows, ldim), F32),
            pltpu.VMEM((batch, ldim), F32),
        ],
        compiler_params=_params("arbitrary"),
        name="rglru_bwd" if reverse else "rglru_fwd",
    )(*args)


def _outproj_e_kernel(c_ref, r_ref, x_ref, pt_ref, w_ref, o_ref, m_ref):
    B, tt, _ = x_ref.shape
    tl = MXU_DIM // B
    cdim = c_ref.shape[1]
    for q in range(tt // tl):
        rows = slice(q * MXU_DIM, (q + 1) * MXU_DIM)
        m_ref[rows, :cdim] = jnp.dot(pt_ref[...], c_ref[rows, :], preferred_element_type=F32).astype(BF16)
        m_ref[rows, cdim:] = jnp.dot(pt_ref[...], r_ref[rows, :], preferred_element_type=F32).astype(BF16)
    acc = jnp.dot(m_ref[...], w_ref[...], preferred_element_type=F32)
    for q in range(tt // tl):
        for b in range(B):
            r0 = q * MXU_DIM + b * tl
            ts = slice(q * tl, (q + 1) * tl)
            o_ref[b, ts, :] = x_ref[b, ts, :] + acc[r0:r0 + tl, :]


def _outproj_e(c, r, x, w):
    B, S, D = x.shape
    cdim, ldim = c.shape[1], r.shape[1]
    tt = _pick_tile(S, 128)
    return pl.pallas_call(
        _outproj_e_kernel,
        grid=(S // tt,),
        in_specs=[
            pl.BlockSpec((tt * B, cdim), lambda i: (i, 0)),
            pl.BlockSpec((tt * B, ldim), lambda i: (i, 0)),
            pl.BlockSpec((B, tt, D), lambda i: (0, i, 0)),
            pl.BlockSpec((MXU_DIM, MXU_DIM), lambda i: (0, 0)),
            pl.BlockSpec((cdim + ldim, D), lambda i: (0, 0)),
        ],
        out_specs=pl.BlockSpec((B, tt, D), lambda i: (0, i, 0)),
        out_shape=jax.ShapeDtypeStruct((B, S, D), F32),
        scratch_shapes=[pltpu.VMEM((tt * B, cdim + ldim), BF16)],
        compiler_params=_params("parallel"),
        name="outproj_even",
    )(c, r, x, _perm_matrix(B).T, w)


FFN_COL_CHUNK = 2 * MXU_DIM


def _ffn_kernel(*refs, bounds):
    n_src = len(bounds) - 1
    x_refs = refs[:n_src]
    g_ref, w1_ref, w3_ref, w2_ref, o_ref, xn_ref = refs[n_src:]
    i = pl.program_id(0)
    for x_ref, lo, hi in zip(x_refs, bounds[:-1], bounds[1:]):
        @pl.when((i >= lo) & (i < hi))
        def _(x_ref=x_ref):
            x = x_ref[...]
            xn_ref[...] = _rmsnorm_f32(x, g_ref[...]).astype(BF16)
            o_ref[...] = x

    xn = xn_ref[...]
    F = w1_ref.shape[1]
    acc = None
    for c0 in range(0, F, FFN_COL_CHUNK):
        c1 = min(c0 + FFN_COL_CHUNK, F)
        a = jnp.dot(xn, w1_ref[:, c0:c1], preferred_element_type=F32)
        b = jnp.dot(xn, w3_ref[:, c0:c1], preferred_element_type=F32)
        act = (a * _sigmoid(a) * b).astype(BF16)
        part = jnp.dot(act, w2_ref[c0:c1, :], preferred_element_type=F32)
        acc = part if acc is None else acc + part
    o_ref[...] += acc


def _ffn(xs, g, w1, w3, w2, *, tm=1024):
    D = xs[0].shape[1]
    F = w1.shape[1]
    assert all(x.shape[0] % tm == 0 for x in xs)
    counts = [x.shape[0] // tm for x in xs]
    starts = [sum(counts[:k]) for k in range(len(xs))]
    T = tm * sum(counts)

    def src_spec(lo, n):
        return pl.BlockSpec((tm, D), lambda i: (jnp.clip(i - lo, 0, n - 1), 0))

    resident = dict(pipeline_mode=pl.Buffered(1))
    return pl.pallas_call(
        functools.partial(_ffn_kernel, bounds=tuple(starts) + (T // tm,)),
        grid=(T // tm,),
        in_specs=[src_spec(lo, n) for lo, n in zip(starts, counts)] + [
            pl.BlockSpec((1, D), lambda i: (0, 0)),
            pl.BlockSpec((D, F), lambda i: (0, 0), **resident),
            pl.BlockSpec((D, F), lambda i: (0, 0), **resident),
            pl.BlockSpec((F, D), lambda i: (0, 0), **resident),
        ],
        out_specs=pl.BlockSpec((tm, D), lambda i: (i, 0)),
        out_shape=jax.ShapeDtypeStruct((T, D), F32),
        scratch_shapes=[pltpu.VMEM((tm, D), BF16)],
        compiler_params=_params("parallel"),
        name="ffn_swiglu",
    )(*xs, g.reshape(1, D), w1, w3, w2)


def _inproj_o_kernel(x_ref, g_ref, w_ref, lg_ref, lb_ref, u_ref, v_ref, zv_ref, *, sdim, n_chunk):
    xn = _rmsnorm_f32(x_ref[...], g_ref[...]).astype(BF16)
    for j in range(sdim // n_chunk):
        cols = slice(j * n_chunk, (j + 1) * n_chunk)
        zu = _gelu_tanh(jnp.dot(xn, w_ref[:, cols], preferred_element_type=F32))
        u_ref[:, cols] = zu.astype(u_ref.dtype)
    for j in range(sdim // n_chunk):
        cols = slice(j * n_chunk, (j + 1) * n_chunk)
        wcols = slice(sdim + j * n_chunk, sdim + (j + 1) * n_chunk)
        zv_ref[:, cols] = _gelu_tanh(jnp.dot(xn, w_ref[:, wcols], preferred_element_type=F32))
    zv = zv_ref[...]
    mu = jnp.mean(zv, axis=-1, keepdims=True)
    d = zv - mu
    var = jnp.mean(d * d, axis=-1, keepdims=True)
    v_ref[...] = (d * lax.rsqrt(var + EPS) * lg_ref[...] + lb_ref[...]).astype(v_ref.dtype)


def _inproj_o(x, g, w, ln_g, ln_b):
    T, D = x.shape
    sdim = w.shape[1] // 2
    tm = _pick_tile(T, 512)
    row = lambda i: (i, 0)
    const = lambda i: (0, 0)
    return pl.pallas_call(
        functools.partial(_inproj_o_kernel, sdim=sdim, n_chunk=_pick_tile(sdim, 512)),
        grid=(T // tm,),
        scratch_shapes=[pltpu.VMEM((tm, sdim), F32)],
        in_specs=[
            pl.BlockSpec((tm, D), row),
            pl.BlockSpec((1, D), const),
            pl.BlockSpec((D, 2 * sdim), const),
            pl.BlockSpec((1, sdim), const),
            pl.BlockSpec((1, sdim), const),
        ],
        out_specs=[pl.BlockSpec((tm, sdim), row), pl.BlockSpec((tm, sdim), row)],
        out_shape=[jax.ShapeDtypeStruct((T, sdim), BF16), jax.ShapeDtypeStruct((T, sdim), BF16)],
        compiler_params=_params("parallel"),
        name="inproj_odd",
    )(x, g.reshape(1, D), w, ln_g.reshape(1, sdim), ln_b.reshape(1, sdim))


def _sgu_out_kernel(u_ref, v_ref, sw_ref, sb_ref, w_ref, x_ref, o_ref, g_ref, *, chunk, heads):
    tm, sdim = u_ref.shape
    hd = sdim // heads
    for n in range(tm // chunk):
        rows = slice(n * chunk, (n + 1) * chunk)
        for h in range(heads):
            lanes = slice(h * hd, (h + 1) * hd)
            sv = jnp.dot(sw_ref[h], v_ref[rows, lanes], preferred_element_type=F32) + sb_ref[h]
            g_ref[rows, lanes] = (u_ref[rows, lanes].astype(F32) * sv).astype(BF16)
    o_ref[...] = x_ref[...] + jnp.dot(g_ref[...], w_ref[...], preferred_element_type=F32)


def _sgu_out(u, v, sgu_w, sgu_b, w, x):
    T, sdim = u.shape
    D = x.shape[1]
    heads, chunk, _ = sgu_w.shape
    hd = sdim // heads
    tm = _pick_tile(T, 512)
    assert tm % chunk == 0
    sb = jnp.broadcast_to(sgu_b.astype(F32)[:, :, None], (heads, chunk, hd))
    row = lambda i: (i, 0)
    return pl.pallas_call(
        functools.partial(_sgu_out_kernel, chunk=chunk, heads=heads),
        grid=(T // tm,),
        in_specs=[
            pl.BlockSpec((tm, sdim), row),
            pl.BlockSpec((tm, sdim), row),
            pl.BlockSpec((heads, chunk, chunk), lambda i: (0, 0, 0)),
            pl.BlockSpec((heads, chunk, hd), lambda i: (0, 0, 0)),
            pl.BlockSpec((sdim, D), lambda i: (0, 0)),
            pl.BlockSpec((tm, D), row),
        ],
        out_specs=pl.BlockSpec((tm, D), row),
        out_shape=jax.ShapeDtypeStruct((T, D), F32),
        scratch_shapes=[pltpu.VMEM((tm, sdim), BF16)],
        compiler_params=_params("parallel"),
        name="sgu_outproj_odd",
    )(u, v, sgu_w, sb, w, x)


META_E1, META_E2, META_R1, META_R2, META_G1, META_G2 = range(6)
ROUTE_TILE = 512


def _router_kernel(x_ref, g_ref, wr_ref, meta_ref, route_ref, cnt_ref, run_ref, *, n_experts):
    @pl.when(pl.program_id(0) == 0)
    def _():
        run_ref[...] = jnp.zeros_like(run_ref)

    xn = _rmsnorm_f32(x_ref[...], g_ref[...])
    xh = xn.astype(BF16)
    xl = (xn - xh.astype(F32)).astype(BF16)
    logits = (jnp.dot(xh, wr_ref[0], preferred_element_type=F32)
              + jnp.dot(xl, wr_ref[0], preferred_element_type=F32)
              + jnp.dot(xh, wr_ref[1], preferred_element_type=F32))
    tm = logits.shape[0]
    lane = lax.broadcasted_iota(jnp.int32, logits.shape, 1).astype(F32)
    neg = jnp.float32(-jnp.inf)
    big = jnp.float32(1e9)
    l1 = jnp.where(lane < n_experts, logits, neg)
    m1 = jnp.max(l1, axis=-1, keepdims=True)
    i1 = jnp.min(jnp.where(l1 == m1, lane, big), axis=-1, keepdims=True)
    l2 = jnp.where(lane == i1, neg, l1)
    m2 = jnp.max(l2, axis=-1, keepdims=True)
    i2 = jnp.min(jnp.where(l2 == m2, lane, big), axis=-1, keepdims=True)
    e2 = jnp.exp(m2 - m1)
    den = 1.0 + e2
    g1 = 1.0 / den
    g2 = e2 / den

    oh1 = jnp.where(lane == i1, 1.0, 0.0)
    oh2 = jnp.where(lane == i2, 1.0, 0.0)
    cnt = oh1 + oh2
    ri = lax.broadcasted_iota(jnp.int32, (tm, tm), 0)
    ci = lax.broadcasted_iota(jnp.int32, (tm, tm), 1)
    lower = jnp.where(ci < ri, 1.0, 0.0).astype(BF16)
    prefix = jnp.dot(lower, cnt.astype(BF16), preferred_element_type=F32) + run_ref[...]
    r1 = jnp.sum(oh1 * prefix, axis=-1, keepdims=True)
    r2 = jnp.sum(oh2 * prefix, axis=-1, keepdims=True)
    run_ref[...] += jnp.sum(cnt, axis=0, keepdims=True)

    meta = jnp.zeros_like(logits)
    for idx, val in ((META_E1, i1), (META_E2, i2), (META_R1, r1), (META_R2, r2), (META_G1, g1), (META_G2, g2)):
        meta = jnp.where(lane == idx, val, meta)
    meta_ref[...] = meta
    route_ref[...] = meta.T[:route_ref.shape[0], :]
    cnt_ref[...] = jnp.broadcast_to(run_ref[...], cnt_ref.shape)


def _router(x, g, w_router):
    T, D = x.shape
    E = w_router.shape[1]
    wr = jnp.zeros((D, LANES), F32).at[:, :E].set(w_router.astype(F32))
    wr_hi = wr.astype(BF16)
    wr = jnp.stack([wr_hi, (wr - wr_hi.astype(F32)).astype(BF16)])
    tm = ROUTE_TILE
    assert T % tm == 0
    row = lambda i: (i, 0)
    return pl.pallas_call(
        functools.partial(_router_kernel, n_experts=E),
        grid=(T // tm,),
        in_specs=[
            pl.BlockSpec((tm, D), row),
            pl.BlockSpec((1, D), lambda i: (0, 0)),
            pl.BlockSpec((2, D, LANES), lambda i: (0, 0, 0)),
        ],
        out_specs=[pl.BlockSpec((tm, LANES), row), pl.BlockSpec((SUBLANES, tm), lambda i: (0, i)),
                   pl.BlockSpec((SUBLANES, LANES), lambda i: (0, 0))],
        out_shape=[jax.ShapeDtypeStruct((T, LANES), F32), jax.ShapeDtypeStruct((SUBLANES, T), F32),
                   jax.ShapeDtypeStruct((SUBLANES, LANES), F32)],
        scratch_shapes=[pltpu.VMEM((1, LANES), F32)],
        compiler_params=_params("arbitrary"),
        name="moe_router",
    )(x, g.reshape(1, D), wr)


def _route_plan(route, cnt, n_experts, tmx, n_tiles):
    counts = cnt[0, :n_experts].astype(jnp.int32)
    nt = (counts + tmx - 1) // tmx
    tend = jnp.cumsum(nt)
    off = (tend - nt) * tmx
    eids = jnp.arange(n_experts, dtype=jnp.int32)

    def pos(e_row, r_row):
        e = route[e_row].astype(jnp.int32)
        base = jnp.sum(jnp.where(e[None, :] == eids[:, None], off[:, None], 0), axis=0)
        return base + route[r_row].astype(jnp.int32)

    T = route.shape[1]
    tm = ROUTE_TILE
    p1 = pos(META_E1, META_R1).reshape(T // tm, 1, tm)
    p2 = pos(META_E2, META_R2).reshape(T // tm, 1, tm)
    tiles = jnp.arange(n_tiles, dtype=jnp.int32)
    tile_expert = jnp.minimum(jnp.sum((tend[None, :] <= tiles[:, None]).astype(jnp.int32), axis=1), n_experts - 1)
    plan = dict(pos=jnp.concatenate([p1, p2], axis=2), tile_expert=tile_expert.astype(jnp.int32),
                n_active=tend[-1:].astype(jnp.int32), pad_start=(off + counts).astype(jnp.int32),
                pad_len=(nt * tmx - counts).astype(jnp.int32))
    return plan


def _dispatch_kernel(pstart_ref, plen_ref, na_ref, pos_ref, x_ref, g_ref, w1_ref, w3_ref, w2_ref,
                     xs_ref, w1o_ref, w3o_ref, w2o_ref, xn_ref, z_ref, sem, zsem,
                     *, pad_bits, n_steps, tmx):
    i = pl.program_id(0)
    tm = x_ref.shape[0]
    slot = i % 2

    third = n_steps // 3
    for k, (src_ref, dst_ref) in enumerate(((w1_ref, w1o_ref), (w3_ref, w3o_ref), (w2_ref, w2o_ref))):
        @pl.when((i >= k * third) & (i < (k + 1) * third))
        def _(src_ref=src_ref, dst_ref=dst_ref):
            dst_ref[...] = src_ref[...].astype(dst_ref.dtype)

    @pl.when(i == 0)
    def _():
        z_ref[...] = jnp.zeros_like(z_ref)

        def zero_rows(dst, rows):
            cp = pltpu.make_async_copy(z_ref.at[pl.ds(0, rows)], xs_ref.at[pl.ds(dst, rows)], zsem)
            cp.start()
            cp.wait()

        for e in range(pstart_ref.shape[0]):
            n = plen_ref[e]
            s = pstart_ref[e]
            head = n & (SUBLANES - 1)
            for j in range(SUBLANES - 1):
                @pl.when(j < head)
                def _(j=j, s=s):
                    zero_rows(s + j, 1)

            groups = n >> 3
            base = s + head
            for k in reversed(range(pad_bits - 3)):
                @pl.when(((groups >> k) & 1) == 1)
                def _(k=k, groups=groups, base=base):
                    dst = pl.multiple_of(base + (((groups >> (k + 1)) << (k + 1)) << 3), SUBLANES)
                    zero_rows(dst, SUBLANES << k)

        zrows = z_ref.shape[0]
        n_tiles = xs_ref.shape[0] // tmx
        for t in range(n_tiles - pstart_ref.shape[0], n_tiles):
            @pl.when(t >= na_ref[0])
            def _(t=t):
                for c in range(tmx // zrows):
                    zero_rows(t * tmx + c * zrows, zrows)

    xn = _rmsnorm_f32(x_ref[...], g_ref[...])

    for s in range(2):
        @pl.when(slot == s)
        def _(s=s):
            xn_ref[s] = xn

            def issue(q, carry):
                r0 = pl.multiple_of(q * SUBLANES, SUBLANES)
                for j in range(SUBLANES):
                    src = xn_ref.at[s, pl.ds(r0 + j, 1)]
                    pltpu.make_async_copy(src, xs_ref.at[pl.ds(pos_ref[0, r0 + j], 1)], sem.at[s]).start(priority=0)
                    pltpu.make_async_copy(src, xs_ref.at[pl.ds(pos_ref[0, tm + r0 + j], 1)], sem.at[s]).start(priority=1)
                return carry

            lax.fori_loop(0, tm // SUBLANES, issue, 0)

    def drain(s):
        for _ in range(2):
            pltpu.make_async_copy(xn_ref.at[s], xs_ref.at[pl.ds(0, tm)], sem.at[s]).wait()

    @pl.when(i > 0)
    def _():
        drain(1 - slot)

    @pl.when(i == n_steps - 1)
    def _():
        drain(slot)


def _dispatch(x, g, plan, n_rows, tmx, w1, w3, w2):
    T, D = x.shape
    tm = ROUTE_TILE
    n = T // tm
    pad_bits = (tmx - 1).bit_length()
    assert n_rows % tmx == 0 and tmx % (1 << (pad_bits - 1)) == 0
    E, _, F = w1.shape
    third = n // 3
    assert n % 3 == 0 and (E * D) % third == 0 and (E * F) % third == 0
    r13, r2 = (E * D) // third, (E * F) // third
    assert r13 % (2 * SUBLANES) == 0 and r2 % (2 * SUBLANES) == 0

    def slab(rows, width, k):
        return pl.BlockSpec((rows, width), lambda i, ps, pn, na: (jnp.clip(i - k * third, 0, third - 1), 0))

    grid_spec = pltpu.PrefetchScalarGridSpec(
        num_scalar_prefetch=3,
        grid=(n,),
        in_specs=[
            pl.BlockSpec((None, 1, 2 * tm), lambda i, ps, pn, na: (i, 0, 0), memory_space=pltpu.SMEM),
            pl.BlockSpec((tm, D), lambda i, ps, pn, na: (i, 0)),
            pl.BlockSpec((1, D), lambda i, ps, pn, na: (0, 0)),
            slab(r13, F, 0), slab(r13, F, 1), slab(r2, D, 2),
        ],
        out_specs=[pl.BlockSpec(memory_space=pl.ANY), slab(r13, F, 0), slab(r13, F, 1), slab(r2, D, 2)],
        scratch_shapes=[
            pltpu.VMEM((2, tm, D), F32),
            pltpu.VMEM((1 << (pad_bits - 1), D), F32),
            pltpu.SemaphoreType.DMA((2,)),
            pltpu.SemaphoreType.DMA(()),
        ],
    )
    xs, w1b, w3b, w2b = pl.pallas_call(
        functools.partial(_dispatch_kernel, pad_bits=pad_bits, n_steps=n, tmx=tmx),
        grid_spec=grid_spec,
        out_shape=[jax.ShapeDtypeStruct((n_rows, D), F32), jax.ShapeDtypeStruct((E * D, F), BF16),
                   jax.ShapeDtypeStruct((E * D, F), BF16), jax.ShapeDtypeStruct((E * F, D), BF16)],
        compiler_params=_params("arbitrary"),
        name="moe_dispatch",
    )(plan["pad_start"], plan["pad_len"], plan["n_active"], plan["pos"], x, g.reshape(1, D),
      w1.reshape(E * D, F), w3.reshape(E * D, F), w2.reshape(E * F, D))
    return xs, w1b.reshape(E, D, F), w3b.reshape(E, D, F), w2b.reshape(E, F, D)


MOE_COL_CHUNK = 4 * MXU_DIM


def _moe_experts_kernel(te_ref, na_ref, x_ref, w1_ref, w3_ref, w2_ref, o_ref, xb_ref):
    del te_ref
    f = pl.program_id(1)
    active = pl.program_id(0) < na_ref[0]

    @pl.when(f == 0)
    def _():
        o_ref[...] = jnp.zeros_like(o_ref)

    @pl.when((f == 0) & active)
    def _():
        xb_ref[...] = x_ref[...].astype(BF16)

    @pl.when(active)
    def _():
        xn = xb_ref[...]
        tf = w1_ref.shape[1]
        acc = None
        for c0 in range(0, tf, MOE_COL_CHUNK):
            c1 = min(c0 + MOE_COL_CHUNK, tf)
            a = jnp.dot(xn, w1_ref[:, c0:c1], preferred_element_type=F32)
            b = jnp.dot(xn, w3_ref[:, c0:c1], preferred_element_type=F32)
            act = (a * _sigmoid(a) * b).astype(BF16)
            part = jnp.dot(act, w2_ref[c0:c1, :], preferred_element_type=F32)
            acc = part if acc is None else acc + part
        o_ref[...] += acc


def _moe_experts(xs, tile_expert, n_active, w1, w3, w2, *, tmx, tf):
    n_rows, D = xs.shape
    E, _, F = w1.shape
    assert F % tf == 0 and n_rows % tmx == 0
    nf = F // tf
    fsel = lambda i, f, na: jnp.where(i < na[0], f, nf - 1)
    grid_spec = pltpu.PrefetchScalarGridSpec(
        num_scalar_prefetch=2,
        grid=(n_rows // tmx, nf),
        in_specs=[
            pl.BlockSpec((tmx, D), lambda i, f, te, na: (jnp.minimum(i, na[0] - 1), 0)),
            pl.BlockSpec((None, D, tf), lambda i, f, te, na: (te[i], 0, fsel(i, f, na))),
            pl.BlockSpec((None, D, tf), lambda i, f, te, na: (te[i], 0, fsel(i, f, na))),
            pl.BlockSpec((None, tf, D), lambda i, f, te, na: (te[i], fsel(i, f, na), 0)),
        ],
        out_specs=pl.BlockSpec((tmx, D), lambda i, f, te, na: (i, 0)),
        scratch_shapes=[pltpu.VMEM((tmx, D), BF16)],
    )
    return pl.pallas_call(
        _moe_experts_kernel,
        grid_spec=grid_spec,
        out_shape=jax.ShapeDtypeStruct((n_rows, D), F32),
        compiler_params=_params("parallel", "arbitrary"),
        name="moe_experts",
    )(tile_expert, n_active, xs, w1, w3, w2)


def _combine_kernel(pos_ref, posn_ref, meta_ref, res_ref, lnf_ref, y_ref, o_ref, ya_ref, yb_ref, sem, *, n_steps):
    i = pl.program_id(0)
    tm = res_ref.shape[0]
    slot = i % 2

    def gather(p_ref, s):
        def issue(q, carry):
            r0 = pl.multiple_of(q * SUBLANES, SUBLANES)
            for j in range(SUBLANES):
                r = r0 + j
                pltpu.make_async_copy(y_ref.at[pl.ds(p_ref[0, r], 1)], ya_ref.at[s, pl.ds(r, 1)],
                                      sem.at[s]).start(priority=0)
                pltpu.make_async_copy(y_ref.at[pl.ds(p_ref[0, tm + r], 1)], yb_ref.at[s, pl.ds(r, 1)],
                                      sem.at[s]).start(priority=1)
            return carry

        lax.fori_loop(0, tm // SUBLANES, issue, 0)

    @pl.when(i == 0)
    def _():
        gather(pos_ref, 0)

    for s in range(2):
        @pl.when((i + 1 < n_steps) & (slot == 1 - s))
        def _(s=s):
            gather(posn_ref, s)

    pltpu.make_async_copy(y_ref.at[pl.ds(0, tm)], ya_ref.at[slot], sem.at[slot]).wait()
    pltpu.make_async_copy(y_ref.at[pl.ds(0, tm)], yb_ref.at[slot], sem.at[slot]).wait()

    meta = meta_ref[...]
    g1 = meta[:, META_G1:META_G1 + 1]
    g2 = meta[:, META_G2:META_G2 + 1]
    h = res_ref[...] + g1 * ya_ref[slot] + g2 * yb_ref[slot]
    o_ref[...] = _rmsnorm_f32(h, lnf_ref[...])


def _combine(y, pos, meta, res, ln_final, t_len, t_off):
    D = res.shape[1]
    tm = ROUTE_TILE
    assert t_len % tm == 0 and t_off % tm == 0
    off = t_off // tm
    last = off + t_len // tm - 1
    return pl.pallas_call(
        functools.partial(_combine_kernel, n_steps=t_len // tm),
        grid=(t_len // tm,),
        in_specs=[
            pl.BlockSpec((None, 1, 2 * tm), lambda i: (off + i, 0, 0), memory_space=pltpu.SMEM),
            pl.BlockSpec((None, 1, 2 * tm), lambda i: (jnp.minimum(off + i + 1, last), 0, 0),
                         memory_space=pltpu.SMEM),
            pl.BlockSpec((tm, LANES), lambda i: (off + i, 0)),
            pl.BlockSpec((tm, D), lambda i: (off + i, 0)),
            pl.BlockSpec((1, D), lambda i: (0, 0)),
            pl.BlockSpec(memory_space=pl.ANY),
        ],
        out_specs=pl.BlockSpec((tm, D), lambda i: (i, 0)),
        out_shape=jax.ShapeDtypeStruct((t_len, D), F32),
        scratch_shapes=[pltpu.VMEM((2, tm, D), F32), pltpu.VMEM((2, tm, D), F32), pltpu.SemaphoreType.DMA((2,))],
        compiler_params=_params("arbitrary"),
        name="moe_combine",
    )(pos, pos, meta, res, ln_final.reshape(1, D), y)


MOE_ROW_TILE = 1024
MOE_FF_TILE = 1792


def _even_mixer(x, p):
    B = x.shape[0]
    cdim = p["dw_conv_w"].shape[-1]
    ldim = p["rg_lam"].shape[-1]
    assert cdim * 2 == ldim and p["w_in_e"].shape[-1] == 2 * cdim + 2 * ldim
    z = _inproj_e(x, p["ln_mix_e"], p["w_in_e"])
    c = _conv_branch(z, p["dw_conv_w"], p["dw_conv_b"], p["conv_ln_g"], p["conv_ln_b"], B)
    lru = functools.partial(_lru_pass, z, batch=B, col_x=2, col_y=1)
    hb = lru(None, p["rg_conv_w"][1], p["rg_conv_b"][1], p["rg_w_a"][1], p["rg_b_a"][1],
             p["rg_w_i"][1], p["rg_b_i"][1], p["rg_lam"][1], reverse=True)
    r = lru(hb, p["rg_conv_w"][0], p["rg_conv_b"][0], p["rg_w_a"][0], p["rg_b_a"][0],
            p["rg_w_i"][0], p["rg_b_i"][0], p["rg_lam"][0], reverse=False)
    return _outproj_e(c, r, x, p["w_out_e"])


def _trunks(xs, p):
    D = xs[0].shape[-1]
    lens = [x.shape[0] * x.shape[1] for x in xs]
    offs = [sum(lens[:k]) for k in range(len(xs))]
    T = sum(lens)

    h = _ffn([_even_mixer(x, p).reshape(n, D) for x, n in zip(xs, lens)],
             p["ln_ffn_e"], p["ffn_w1"], p["ffn_w3"], p["ffn_w2"])

    u, v = _inproj_o(h, p["ln_mix_o"], p["w_in_o"], p["sgu_ln_g"], p["sgu_ln_b"])
    h = _sgu_out(u, v, p["sgu_w"], p["sgu_b"], p["w_out_o"], h)

    E = p["w_router"].shape[1]
    tmx = MOE_ROW_TILE
    tf = MOE_FF_TILE if p["moe_w1"].shape[-1] % MOE_FF_TILE == 0 else _pick_tile(p["moe_w1"].shape[-1], 512)
    n_tiles = (2 * T) // tmx + E
    meta, route, cnt = _router(h, p["ln_ffn_o"], p["w_router"])
    plan = _route_plan(route, cnt, E, tmx, n_tiles)
    xsort, w1, w3, w2 = _dispatch(h, p["ln_ffn_o"], plan, n_tiles * tmx, tmx,
                                  p["moe_w1"], p["moe_w3"], p["moe_w2"])
    y = _moe_experts(xsort, plan["tile_expert"], plan["n_active"], w1, w3, w2, tmx=tmx, tf=tf)
    return tuple(_combine(y, plan["pos"], meta, h, p["ln_final"], n, off).reshape(x.shape)
                 for x, n, off in zip(xs, lens, offs))


def kernel(x_prompt, x_sample, ln_mix_e, w_in_e, dw_conv_w, dw_conv_b, conv_ln_g, conv_ln_b, rg_conv_w, rg_conv_b, rg_w_a, rg_b_a, rg_w_i, rg_b_i, rg_lam, w_out_e, ln_ffn_e, ffn_w1, ffn_w3, ffn_w2, ln_mix_o, w_in_o, sgu_ln_g, sgu_ln_b, sgu_w, sgu_b, w_out_o, ln_ffn_o, w_router, moe_w1, moe_w3, moe_w2, ln_final):
    assert ln_mix_e.shape[0] == 1 and ln_mix_o.shape[0] == 1, "one even and one odd layer"
    bf = lambda a: a[0].astype(BF16)
    f32 = lambda a: a[0].astype(F32)
    p = {
        "ln_mix_e": f32(ln_mix_e), "w_in_e": bf(w_in_e),
        "dw_conv_w": f32(dw_conv_w), "dw_conv_b": f32(dw_conv_b),
        "conv_ln_g": f32(conv_ln_g), "conv_ln_b": f32(conv_ln_b),
        "rg_conv_w": f32(rg_conv_w), "rg_conv_b": f32(rg_conv_b),
        "rg_w_a": f32(rg_w_a), "rg_b_a": f32(rg_b_a), "rg_w_i": f32(rg_w_i), "rg_b_i": f32(rg_b_i),
        "rg_lam": f32(rg_lam), "w_out_e": bf(w_out_e),
        "ln_ffn_e": f32(ln_ffn_e), "ffn_w1": bf(ffn_w1), "ffn_w3": bf(ffn_w3), "ffn_w2": bf(ffn_w2),
        "ln_mix_o": f32(ln_mix_o), "w_in_o": bf(w_in_o),
        "sgu_ln_g": f32(sgu_ln_g), "sgu_ln_b": f32(sgu_ln_b),
        "sgu_w": bf(sgu_w), "sgu_b": f32(sgu_b), "w_out_o": bf(w_out_o),
        "ln_ffn_o": f32(ln_ffn_o), "w_router": f32(w_router),
        "moe_w1": f32(moe_w1), "moe_w3": f32(moe_w3), "moe_w2": f32(moe_w2),
        "ln_final": ln_final.astype(F32),
    }
    return _trunks([x_prompt, x_sample], p)
```
